```python
import math
import jax, jax.numpy as jnp
from jax import lax
import numpy as np

D_MODEL = 2048
BATCH = 1
SEQ = 8192
DEPTH = 1
DEC_BATCH = 32
DEC_SEQ = 1
PAST_LEN = 8192
PAGE_SIZE = 128

N_HEADS_A = 8
D_HEAD_A = 64
WIDTH_A = 2 * N_HEADS_A * D_HEAD_A
N_HEADS_B = 16
D_HEAD_B = 64
WIDTH_B = N_HEADS_B * D_HEAD_B
R_DECAY = 96
R_AAA = 96
R_GATE = 256
D_FF = -(-(8 * D_MODEL) // (3 * 256)) * 256
ROPE_THETA = 10000.0
Q_BLOCK = 128
NORM_EPS = 1e-6
SUBLN_EPS = 1e-5
GN_EPS = 64e-5
A_COLS = 3 * WIDTH_A
RW_COLS = 3 * WIDTH_B + R_DECAY + R_AAA + R_GATE
GATE_COLS = 2 * D_MODEL
IN_COLS = A_COLS + RW_COLS + GATE_COLS

kernel_name = "hybrid_diffattn_rwkv7_decoder_step"


def rms_norm(x, g, eps=NORM_EPS):
    xf = x.astype(jnp.float32)
    y = xf * lax.rsqrt(jnp.mean(xf * xf, axis=-1, keepdims=True) + eps)
    return (y * g.astype(jnp.float32)).astype(x.dtype)


def rope(x, pos):
    half = x.shape[-1] // 2
    inv = 1.0 / (ROPE_THETA ** (jnp.arange(half, dtype=jnp.float32) / half))
    ang = pos.astype(jnp.float32)[:, None] * inv[None, :]
    cos = jnp.cos(ang)[None, :, None, :]
    sin = jnp.sin(ang)[None, :, None, :]
    xf = x.astype(jnp.float32)
    x1, x2 = xf[..., :half], xf[..., half:]
    return jnp.concatenate([x1 * cos - x2 * sin, x2 * cos + x1 * sin], axis=-1).astype(x.dtype)


def mixer_projections(h, w_in, pos):
    b, t = h.shape[0], h.shape[1]
    p = h @ w_in
    q = rope(p[..., :WIDTH_A].reshape(b, t, 2 * N_HEADS_A, D_HEAD_A), pos)
    k = rope(p[..., WIDTH_A:2 * WIDTH_A].reshape(b, t, 2 * N_HEADS_A, D_HEAD_A), pos)
    v = p[..., 2 * WIDTH_A:A_COLS].reshape(b, t, N_HEADS_A, 2 * D_HEAD_A)
    rw = p[..., A_COLS:A_COLS + RW_COLS]
    ga = p[..., A_COLS + RW_COLS:A_COLS + RW_COLS + D_MODEL]
    gb = p[..., A_COLS + RW_COLS + D_MODEL:]
    return q, k, v, rw, ga, gb


def diff_lambda(lq1, lk1, lq2, lk2, lambda_init):
    f = lambda a: a.astype(jnp.float32)
    return jnp.exp(jnp.sum(f(lq1) * f(lk1))) - jnp.exp(jnp.sum(f(lq2) * f(lk2))) + lambda_init


def diff_attn_core(q, k, v, q_pos, k_pos, lam):
    s = jnp.einsum('bqhd,bkhd->bhqk', q, k).astype(jnp.float32) * (D_HEAD_A ** -0.5)
    mask = k_pos[None, :] <= q_pos[:, None]
    s = jnp.where(mask[None, None], s, -jnp.inf)
    p = jax.nn.softmax(s, axis=-1)
    b, _, tq, tk = p.shape
    p = p.reshape(b, N_HEADS_A, 2, tq, tk)
    comb = p[:, :, 0] - lam * p[:, :, 1]
    return jnp.einsum('bhqk,bkhe->bqhe', comb.astype(v.dtype), v)


def prompt_diff_attention(q, k, v, lam):
    b, s = q.shape[0], q.shape[1]
    k_pos = jnp.arange(s)

    def blk(i):
        q_blk = lax.dynamic_slice_in_dim(q, i * Q_BLOCK, Q_BLOCK, axis=1)
        q_pos = i * Q_BLOCK + jnp.arange(Q_BLOCK)
        return diff_attn_core(q_blk, k, v, q_pos, k_pos, lam)

    o = lax.map(blk, jnp.arange(s // Q_BLOCK))
    return jnp.moveaxis(o, 0, 1).reshape(b, s, N_HEADS_A, 2 * D_HEAD_A)


def diff_attn_out(o, subln_w, lambda_init):
    o = rms_norm(o, subln_w, SUBLN_EPS) * (1.0 - lambda_init)
    return o.reshape(o.shape[0], o.shape[1], WIDTH_A)


def rwkv7_scan(s0, r, w, k, v, a, b):
    xs = tuple(jnp.moveaxis(t.astype(jnp.float32), 1, 0) for t in (r, w, k, v, a, b))

    def step(S, inp):
        r_t, w_t, k_t, v_t, a_t, b_t = inp
        sa = jnp.einsum('bhij,bhj->bhi', S, a_t)
        S = S * w_t[:, :, None, :] + sa[..., None] * b_t[:, :, None, :] + v_t[..., None] * k_t[:, :, None, :]
        y = jnp.einsum('bhij,bhj->bhi', S, r_t)
        return S, y

    S, ys = lax.scan(step, s0.astype(jnp.float32), xs)
    return S, jnp.moveaxis(ys, 0, 1)


def rwkv7_branch(cols, prev, s0, mu, w0, w_decay, a0, w_aaa, w_gate_lora, k_k, k_a, r_k, lnx_w, lnx_b):
    bsz, t = cols.shape[0], cols.shape[1]
    prev_seq = jnp.concatenate([prev[:, None, :].astype(cols.dtype), cols[:, :-1]], axis=1)
    u = cols + mu * (prev_seq - cols)
    r = u[..., :WIDTH_B]
    kx = u[..., WIDTH_B:2 * WIDTH_B]
    v = u[..., 2 * WIDTH_B:3 * WIDTH_B]
    o = 3 * WIDTH_B
    wd = u[..., o:o + R_DECAY]
    ad = u[..., o + R_DECAY:o + R_DECAY + R_AAA]
    gd = u[..., o + R_DECAY + R_AAA:]
    w_raw = -jax.nn.softplus(-(w0 + jnp.tanh(wd) @ w_decay)) - 0.5
    decay = jnp.exp(-jnp.exp(w_raw.astype(jnp.float32)))
    a = jax.nn.sigmoid(a0 + ad @ w_aaa)
    g = jax.nn.sigmoid(gd) @ w_gate_lora
    heads = lambda z: z.reshape(bsz, t, N_HEADS_B, D_HEAD_B)
    kk = heads(kx * k_k).astype(jnp.float32)
    kk = kk / jnp.maximum(jnp.sqrt(jnp.sum(kk * kk, axis=-1, keepdims=True)), 1e-12)
    kx = kx * (1.0 + (a - 1.0) * k_a)
    rh, kh, vh, ah = heads(r), heads(kx), heads(v), heads(a)
    S, y = rwkv7_scan(s0, rh, heads(decay), kh, vh, -kk, kk * ah)
    mean = jnp.mean(y, axis=-1, keepdims=True)
    var = jnp.mean(jnp.square(y - mean), axis=-1, keepdims=True)
    y = ((y - mean) * lax.rsqrt(var + GN_EPS)).reshape(bsz, t, WIDTH_B)
    y = y * lnx_w.astype(jnp.float32) + lnx_b.astype(jnp.float32)
    bonus = jnp.sum((rh * kh * r_k).astype(jnp.float32), axis=-1, keepdims=True) * vh.astype(jnp.float32)
    y = (y + bonus.reshape(bsz, t, WIDTH_B)) * g.astype(jnp.float32)
    return y.astype(cols.dtype), S.astype(s0.dtype), cols[:, -1]


def setup_inputs(seed: int = 0) -> dict:
    key = jax.random.key(seed)
    ks = iter(jax.random.split(key, 48))
    f32 = jnp.float32
    L = DEPTH

    def nrm(shape, scale):
        return jax.random.normal(next(ks), shape, f32) * scale

    def gain(shape):
        return 1.0 + nrm(shape, 0.02)

    n_pages = PAST_LEN // PAGE_SIZE
    n_used = DEC_BATCH * n_pages
    n_pool = n_used + max(1, n_used // 4)
    x_prompt = nrm((BATCH, SEQ, D_MODEL), 1.0)
    x_sample = nrm((DEC_BATCH, DEC_SEQ, D_MODEL), 1.0)
    cache_k = nrm((L, n_pool, PAGE_SIZE, 2 * N_HEADS_A, D_HEAD_A), 1.0)
    cache_v = nrm((L, n_pool, PAGE_SIZE, N_HEADS_A, 2 * D_HEAD_A), 1.0)
    state_rwkv = nrm((L, DEC_BATCH, N_HEADS_B, D_HEAD_B, D_HEAD_B), 0.3)
    state_shift = nrm((L, DEC_BATCH, RW_COLS), 1.0)
    page_table = jax.random.permutation(next(ks), n_pool)[:n_used].reshape(DEC_BATCH, n_pages).astype(jnp.int32)
    return {
        "x_prompt": x_prompt,
        "x_sample": x_sample,
        "cache_k": cache_k,
        "cache_v": cache_v,
        "state_rwkv": state_rwkv,
        "state_shift": state_shift,
        "page_table": page_table,
        "norm_mix_pre": gain((L, D_MODEL)),
        "w_in": nrm((L, D_MODEL, IN_COLS), D_MODEL ** -0.5),
        "lambda_q1": nrm((L, D_HEAD_A), 0.1),
        "lambda_k1": nrm((L, D_HEAD_A), 0.1),
        "lambda_q2": nrm((L, D_HEAD_A), 0.1),
        "lambda_k2": nrm((L, D_HEAD_A), 0.1),
        "subln_w": gain((L, 2 * D_HEAD_A)),
        "rw_mu": jax.random.uniform(next(ks), (L, RW_COLS), f32),
        "w0": jax.random.uniform(next(ks), (L, WIDTH_B), f32, -4.0, 1.0),
        "w_decay": nrm((L, R_DECAY, WIDTH_B), 0.1 * R_DECAY ** -0.5),
        "a0": nrm((L, WIDTH_B), 0.1),
        "w_aaa": nrm((L, R_AAA, WIDTH_B), 0.5 * R_AAA ** -0.5),
        "w_gate_lora": nrm((L, R_GATE, WIDTH_B), R_GATE ** -0.5),
        "k_k": 0.85 + nrm((L, WIDTH_B), 0.05),
        "k_a": 1.0 + nrm((L, WIDTH_B), 0.05),
        "r_k": nrm((L, N_HEADS_B, D_HEAD_B), 0.1),
        "lnx_w": gain((L, WIDTH_B)),
        "lnx_b": nrm((L, WIDTH_B), 0.02),
        "w_branch_a": nrm((L, WIDTH_A, D_MODEL), WIDTH_A ** -0.5),
        "w_branch_b": nrm((L, WIDTH_B, D_MODEL), WIDTH_B ** -0.5),
        "w_out": nrm((L, D_MODEL, D_MODEL), D_MODEL ** -0.5),
        "norm_mix_post": gain((L, D_MODEL)),
        "norm_ffn_pre": gain((L, D_MODEL)),
        "w_ffn_gate": nrm((L, D_MODEL, D_FF), D_MODEL ** -0.5),
        "w_ffn_up": nrm((L, D_MODEL, D_FF), D_MODEL ** -0.5),
        "w_ffn_down": nrm((L, D_FF, D_MODEL), D_FF ** -0.5),
        "norm_ffn_post": gain((L, D_MODEL)),
    }


def reference(x_prompt, x_sample, cache_k, cache_v, state_rwkv, state_shift, page_table,
              norm_mix_pre, w_in, lambda_q1, lambda_k1, lambda_q2, lambda_k2, subln_w,
              rw_mu, w0, w_decay, a0, w_aaa, w_gate_lora, k_k, k_a, r_k, lnx_w, lnx_b,
              w_branch_a, w_branch_b, w_out, norm_mix_post, norm_ffn_pre,
              w_ffn_gate, w_ffn_up, w_ffn_down, norm_ffn_post):
    b, s = x_prompt.shape[0], x_prompt.shape[1]
    db, ds = x_sample.shape[0], x_sample.shape[1]
    past = page_table.shape[1] * cache_k.shape[2]
    pos_p = jnp.arange(s)
    pos_s = past + jnp.arange(ds)
    k_pos_s = jnp.arange(past + ds)

    xp, xs = x_prompt, x_sample
    kp_l, vp_l, sp_l, shp_l = [], [], [], []
    ks_l, vs_l, ss_l, shs_l = [], [], [], []
    for l in range(DEPTH):
        lambda_init = 0.8 - 0.6 * math.exp(-0.3 * l)
        lam = diff_lambda(lambda_q1[l], lambda_k1[l], lambda_q2[l], lambda_k2[l], lambda_init)

        def rwkv(cols, prev, s0):
            return rwkv7_branch(cols, prev, s0, rw_mu[l], w0[l], w_decay[l], a0[l], w_aaa[l],
                                w_gate_lora[l], k_k[l], k_a[l], r_k[l], lnx_w[l], lnx_b[l])

        def merge_and_ffn(x, att, rwk, ga, gb):
            ya = att @ w_branch_a[l]
            yb = rwk @ w_branch_b[l]
            m = jax.nn.sigmoid(ga) * ya + jax.nn.sigmoid(gb) * yb
            x = x + rms_norm(m @ w_out[l], norm_mix_post[l])
            hf = rms_norm(x, norm_ffn_pre[l])
            f = (jax.nn.silu(hf @ w_ffn_gate[l]) * (hf @ w_ffn_up[l])) @ w_ffn_down[l]
            return x + rms_norm(f, norm_ffn_post[l])

        h = rms_norm(xp, norm_mix_pre[l])
        q, k, v, rw, ga, gb = mixer_projections(h, w_in[l], pos_p)
        att = diff_attn_out(prompt_diff_attention(q, k, v, lam), subln_w[l], lambda_init)
        s0 = jnp.zeros((b, N_HEADS_B, D_HEAD_B, D_HEAD_B), jnp.float32)
        rwk, S_p, sh_p = rwkv(rw, jnp.zeros((b, RW_COLS), rw.dtype), s0)
        xp = merge_and_ffn(xp, att, rwk, ga, gb)
        kp_l.append(k); vp_l.append(v); sp_l.append(S_p); shp_l.append(sh_p)

        h = rms_norm(xs, norm_mix_pre[l])
        q, k, v, rw, ga, gb = mixer_projections(h, w_in[l], pos_s)
        k_past = cache_k[l, page_table].reshape(db, past, 2 * N_HEADS_A, D_HEAD_A)
        v_past = cache_v[l, page_table].reshape(db, past, N_HEADS_A, 2 * D_HEAD_A)
        k_all = jnp.concatenate([k_past, k.astype(k_past.dtype)], axis=1)
        v_all = jnp.concatenate([v_past, v.astype(v_past.dtype)], axis=1)
        o = diff_attn_core(q, k_all, v_all, pos_s, k_pos_s, lam)
        att = diff_attn_out(o, subln_w[l], lambda_init)
        rwk, S_s, sh_s = rwkv(rw, state_shift[l], state_rwkv[l])
        xs = merge_and_ffn(xs, att, rwk, ga, gb)
        ks_l.append(k); vs_l.append(v); ss_l.append(S_s); shs_l.append(sh_s)

    return (xp, xs,
            jnp.stack(kp_l), jnp.stack(vp_l), jnp.stack(sp_l), jnp.stack(shp_l),
            jnp.stack(ks_l), jnp.stack(vs_l), jnp.stack(ss_l), jnp.stack(shs_l))
```

```python
import functools
import math

import jax
import jax.numpy as jnp
from jax import lax
from jax.experimental import pallas as pl
from jax.experimental.pallas import tpu as pltpu

F32 = jnp.float32
BF16 = jnp.bfloat16

D_MODEL = 2048
N_HEADS_A = 8
D_HEAD_A = 64
WIDTH_A = 2 * N_HEADS_A * D_HEAD_A
N_HEADS_B = 16
D_HEAD_B = 64
WIDTH_B = N_HEADS_B * D_HEAD_B
R_DECAY = 96
R_AAA = 96
R_GATE = 256
ROPE_THETA = 10000.0
NORM_EPS = 1e-6
SUBLN_EPS = 1e-5
GN_EPS = 64e-5
PAGE_SIZE = 128
A_COLS = 3 * WIDTH_A
RW_COLS = 3 * WIDTH_B + R_DECAY + R_AAA + R_GATE

LANES = 128
LORA_PAD = 128
RW_PAD_COLS = 3 * WIDTH_B + 2 * LORA_PAD + R_GATE
CHUNK = 64
INV_BLOCK = 16
VMEM_LIMIT = 56 * 1024 * 1024


def _cparams(sem):
    return pltpu.CompilerParams(dimension_semantics=sem, vmem_limit_bytes=VMEM_LIMIT)


def _dot(a, b):
    return jnp.dot(a, b, preferred_element_type=F32)


def _dot_nt(a, b):
    return lax.dot_general(a, b, (((1,), (1,)), ((), ())), preferred_element_type=F32)


def _dot_tn(a, b):
    return lax.dot_general(a, b, (((0,), (0,)), ((), ())), preferred_element_type=F32)


def _sigmoid(x):
    return 1.0 / (1.0 + jnp.exp(-x))


def _rms(x, g, eps):
    return x * lax.rsqrt(jnp.mean(x * x, axis=-1, keepdims=True) + eps) * g


def _norm_kernel(x_ref, g_ref, o_ref):
    o_ref[...] = _rms(x_ref[...], g_ref[...], NORM_EPS).astype(o_ref.dtype)


def rmsnorm_bf16(x, g, tm):
    m, d = x.shape
    return pl.pallas_call(
        _norm_kernel,
        grid=(m // tm,),
        in_specs=[pl.BlockSpec((tm, d), lambda i: (i, 0)),
                  pl.BlockSpec((1, d), lambda i: (0, 0))],
        out_specs=pl.BlockSpec((tm, d), lambda i: (i, 0)),
        out_shape=jax.ShapeDtypeStruct((m, d), BF16),
        compiler_params=_cparams(("parallel",)),
        name="rmsnorm_bf16",
    )(x, g.reshape(1, d))


def _rope_tile(x, cos, sin_signed):
    lane = lax.broadcasted_iota(jnp.int32, x.shape, 1)
    first_half = (lane % D_HEAD_A) < (D_HEAD_A // 2)
    partner = jnp.where(first_half,
                        pltpu.roll(x, LANES - D_HEAD_A // 2, 1),
                        pltpu.roll(x, D_HEAD_A // 2, 1))
    return x * cos + partner * sin_signed


def _proj_kernel(*refs, rope, scale, n_out):
    if rope:
        a_ref, w_ref, cos_ref, sin_ref = refs[:4]
        outs = refs[4:]
    else:
        a_ref, w_ref = refs[:2]
        outs = refs[2:]
    acc = _dot(a_ref[...], w_ref[...])
    if rope:
        cos = cos_ref[...]
        sin = sin_ref[...]
        tn = acc.shape[1]
        acc = jnp.concatenate(
            [_rope_tile(acc[:, c * LANES:(c + 1) * LANES], cos, sin) for c in range(tn // LANES)], axis=1)
    if scale != 1.0:
        acc = acc * scale
    for o_ref in outs[:n_out]:
        o_ref[...] = acc.astype(o_ref.dtype)


def projection(a, w, out_dtypes, tm, tn, rope_tables=None, scale=1.0, name="projection"):
    m, kdim = a.shape
    n = w.shape[1]
    in_specs = [pl.BlockSpec((tm, kdim), lambda i, j: (i, 0)),
                pl.BlockSpec((kdim, tn), lambda i, j: (0, j))]
    args = [a, w]
    if rope_tables is not None:
        in_specs += [pl.BlockSpec((tm, LANES), lambda i, j: (i, 0))] * 2
        args += list(rope_tables)
    outs = pl.pallas_call(
        functools.partial(_proj_kernel, rope=rope_tables is not None, scale=scale, n_out=len(out_dtypes)),
        grid=(m // tm, n // tn),
        in_specs=in_specs,
        out_specs=[pl.BlockSpec((tm, tn), lambda i, j: (i, j)) for _ in out_dtypes],
        out_shape=[jax.ShapeDtypeStruct((m, n), dt) for dt in out_dtypes],
        compiler_params=_cparams(("parallel", "arbitrary")),
        name=name,
    )(*args)
    return outs


def rope_tables(pos):
    half = D_HEAD_A // 2
    inv = 1.0 / (ROPE_THETA ** (jnp.arange(half, dtype=F32) / half))
    ang = pos.astype(F32)[:, None] * inv[None, :]
    cos = jnp.cos(ang)
    sin = jnp.sin(ang)
    cos2 = jnp.concatenate([cos, cos, cos, cos], axis=1)
    sin2 = jnp.concatenate([-sin, sin, -sin, sin], axis=1)
    return cos2, sin2


def _diff_lambda(lq1, lk1, lq2, lk2, lambda_init):
    s1 = jnp.sum(lq1 * lk1, axis=-1, keepdims=True)
    s2 = jnp.sum(lq2 * lk2, axis=-1, keepdims=True)
    return jnp.exp(s1) - jnp.exp(s2) + lambda_init


def _flash_kernel(q_ref, k_ref, v_ref, lq1_ref, lk1_ref, lq2_ref, lk2_ref, g_ref, o_ref,
                  qs_ref, m_ref, l_ref, acc_ref, *, tq, tk, lambda_init):
    qi = pl.program_id(1)
    kv = pl.program_id(2)
    n_kv = (qi + 1) * (tq // tk)

    @pl.when(kv == 0)
    def _():
        q = q_ref[...]
        lane = lax.broadcasted_iota(jnp.int32, q.shape, 1)
        zero = jnp.zeros_like(q)
        qs_ref[0:tq, :] = jnp.where(lane < D_HEAD_A, q, zero)
        qs_ref[tq:2 * tq, :] = jnp.where(lane >= D_HEAD_A, q, zero)
        m_ref[...] = jnp.full(m_ref.shape, -jnp.inf, F32)
        l_ref[...] = jnp.zeros(l_ref.shape, F32)
        acc_ref[...] = jnp.zeros(acc_ref.shape, F32)

    def step(masked):
        s = _dot_nt(qs_ref[...], k_ref[...])
        if masked:
            row = lax.broadcasted_iota(jnp.int32, s.shape, 0)
            col = lax.broadcasted_iota(jnp.int32, s.shape, 1)
            q_pos = qi * tq + jnp.where(row >= tq, row - tq, row)
            k_pos = kv * tk + col
            s = jnp.where(k_pos <= q_pos, s, -jnp.inf)
        m_prev = m_ref[...]
        m_new = jnp.maximum(m_prev, jnp.max(s, axis=-1, keepdims=True))
        alpha = jnp.exp(m_prev - m_new)
        p = jnp.exp(s - m_new)
        l_ref[...] = alpha * l_ref[...] + jnp.sum(p, axis=-1, keepdims=True)
        acc_ref[...] = alpha * acc_ref[...] + _dot(p.astype(BF16), v_ref[...])
        m_ref[...] = m_new

    diag_start = qi * (tq // tk)

    @pl.when(kv < diag_start)
    def _():
        step(False)

    @pl.when(jnp.logical_and(kv >= diag_start, kv < n_kv))
    def _():
        step(True)

    @pl.when(kv == n_kv - 1)
    def _():
        lam = _diff_lambda(lq1_ref[...], lk1_ref[...], lq2_ref[...], lk2_ref[...], lambda_init)
        o = acc_ref[...] / l_ref[...]
        o = o[0:tq, :] - lam * o[tq:2 * tq, :]
        o = _rms(o, g_ref[...], SUBLN_EPS) * (1.0 - lambda_init)
        o_ref[...] = o.astype(o_ref.dtype)


def prompt_diff_attention(q, k, v, lam_params, subln_w, lambda_init, tq, tk):
    t = q.shape[0]
    n_pairs = N_HEADS_A
    nq = t // tq
    nk = t // tk
    ratio = tq // tk

    def kv_map(h, i, j):
        return (jnp.minimum(j, (i + 1) * ratio - 1), h)

    small = pl.BlockSpec((1, D_HEAD_A), lambda h, i, j: (0, 0))
    return pl.pallas_call(
        functools.partial(_flash_kernel, tq=tq, tk=tk, lambda_init=lambda_init),
        grid=(n_pairs, nq, nk),
        in_specs=[pl.BlockSpec((tq, LANES), lambda h, i, j: (i, h)),
                  pl.BlockSpec((tk, LANES), kv_map),
                  pl.BlockSpec((tk, LANES), kv_map),
                  small, small, small, small,
                  pl.BlockSpec((1, LANES), lambda h, i, j: (0, 0))],
        out_specs=pl.BlockSpec((tq, LANES), lambda h, i, j: (i, h)),
        out_shape=jax.ShapeDtypeStruct((t, WIDTH_A), BF16),
        scratch_shapes=[pltpu.VMEM((2 * tq, LANES), BF16),
                        pltpu.VMEM((2 * tq, 1), F32),
                        pltpu.VMEM((2 * tq, 1), F32),
                        pltpu.VMEM((2 * tq, LANES), F32)],
        compiler_params=_cparams(("parallel", "parallel", "arbitrary")),
        name="prompt_diff_attention",
    )(q, k, v, *lam_params, subln_w.reshape(1, LANES))


def _decode_kernel(pt_ref, q_ref, kn_ref, vn_ref, lq1_ref, lk1_ref, lq2_ref, lk2_ref, g_ref, *rest,
                   pages_per_step, lambda_init):
    del pt_ref
    k_refs = rest[:pages_per_step]
    v_refs = rest[pages_per_step:2 * pages_per_step]
    o_ref = rest[2 * pages_per_step]
    qm_ref, ex_ref, m_ref, l_ref, acc_ref = rest[2 * pages_per_step + 1:]
    hp = N_HEADS_A
    n_rows = 2 * hp
    j = pl.program_id(1)

    def pair_mask():
        row = lax.broadcasted_iota(jnp.int32, (n_rows, PAGE_SIZE * hp), 0)
        lane = lax.broadcasted_iota(jnp.int32, (n_rows, PAGE_SIZE * hp), 1)
        return (lane % hp) == (row % hp)

    @pl.when(j == 0)
    def _():
        row = lax.broadcasted_iota(jnp.int32, (n_rows, WIDTH_A), 0)
        lane = lax.broadcasted_iota(jnp.int32, (n_rows, WIDTH_A), 1)
        qb = jnp.broadcast_to(q_ref[0], (n_rows, WIDTH_A))
        qm_ref[...] = jnp.where(lane // D_HEAD_A == 2 * (row % hp) + row // hp, qb, 0.0)
        tok = lax.broadcasted_iota(jnp.int32, ex_ref.shape, 0)
        pos = lax.broadcasted_iota(jnp.int32, ex_ref.shape, 1)
        ex_ref[...] = jnp.where(pos // hp == tok, 1.0, 0.0).astype(BF16)
        m_ref[...] = jnp.full(m_ref.shape, -jnp.inf, F32)
        l_ref[...] = jnp.zeros(l_ref.shape, F32)
        acc_ref[...] = jnp.zeros(acc_ref.shape, F32)

    qm = qm_ref[...].astype(BF16)
    keep = pair_mask()
    for i in range(pages_per_step):
        kp = k_refs[i][0].astype(BF16)
        vp = v_refs[i][0].reshape(PAGE_SIZE * hp, LANES).astype(BF16)
        s = _dot(qm, kp)
        m_prev = m_ref[...]
        m_new = jnp.maximum(m_prev, jnp.max(s, axis=-1, keepdims=True))
        alpha = jnp.exp(m_prev - m_new)
        p = jnp.exp(s - m_new)
        l_ref[...] = alpha * l_ref[...] + jnp.sum(p, axis=-1, keepdims=True)
        p_wide = jnp.where(keep, _dot(p.astype(BF16), ex_ref[...]), 0.0).astype(BF16)
        acc_ref[...] = alpha * acc_ref[...] + _dot(p_wide, vp)
        m_ref[...] = m_new

    @pl.when(j == pl.num_programs(1) - 1)
    def _():
        s = jnp.sum(qm_ref[...] * kn_ref[0], axis=-1, keepdims=True)
        m_prev = m_ref[...]
        m_new = jnp.maximum(m_prev, s)
        alpha = jnp.exp(m_prev - m_new)
        p = jnp.exp(s - m_new)
        l = alpha * l_ref[...] + p
        vn = vn_ref[0]
        acc = alpha * acc_ref[...] + p * jnp.concatenate([vn, vn], axis=0)
        lam = _diff_lambda(lq1_ref[...], lk1_ref[...], lq2_ref[...], lk2_ref[...], lambda_init)
        o = acc / l
        o = o[0:hp, :] - lam * o[hp:n_rows, :]
        o_ref[0] = (_rms(o, g_ref[...], SUBLN_EPS) * (1.0 - lambda_init)).astype(o_ref.dtype)


def sample_diff_attention(q, k_new, v_new, cache_k, cache_v, page_table, lam_params, subln_w,
                          lambda_init, pages_per_step):
    b = q.shape[0]
    n_pages = page_table.shape[1]
    steps = n_pages // pages_per_step
    hp = N_HEADS_A
    row3 = lambda x: x.reshape(b, 1, WIDTH_A)
    vec = pl.BlockSpec((1, 1, WIDTH_A), lambda bi, j, pt: (bi, 0, 0))
    small = pl.BlockSpec((1, D_HEAD_A), lambda bi, j, pt: (0, 0))
    pairs = pl.BlockSpec((1, hp, LANES), lambda bi, j, pt: (bi, 0, 0))

    def page_spec(i, shape):
        zeros = (0,) * len(shape)
        return pl.BlockSpec((1,) + shape,
                            lambda bi, j, pt: (pt[bi * n_pages + j * pages_per_step + i],) + zeros)

    grid_spec = pltpu.PrefetchScalarGridSpec(
        num_scalar_prefetch=1,
        grid=(b, steps),
        in_specs=[vec, vec, pairs, small, small, small, small,
                  pl.BlockSpec((1, LANES), lambda bi, j, pt: (0, 0))]
                 + [page_spec(i, (WIDTH_A, PAGE_SIZE)) for i in range(pages_per_step)]
                 + [page_spec(i, (PAGE_SIZE, hp, LANES)) for i in range(pages_per_step)],
        out_specs=pairs,
        scratch_shapes=[pltpu.VMEM((2 * hp, WIDTH_A), F32),
                        pltpu.VMEM((PAGE_SIZE, PAGE_SIZE * hp), BF16),
                        pltpu.VMEM((2 * hp, 1), F32),
                        pltpu.VMEM((2 * hp, 1), F32),
                        pltpu.VMEM((2 * hp, LANES), F32)],
    )
    out = pl.pallas_call(
        functools.partial(_decode_kernel, pages_per_step=pages_per_step, lambda_init=lambda_init),
        grid_spec=grid_spec,
        out_shape=jax.ShapeDtypeStruct((b, hp, LANES), BF16),
        compiler_params=_cparams(("parallel", "arbitrary")),
        name="sample_diff_attention",
    )(page_table.reshape(-1), row3(q), row3(k_new), v_new.reshape(b, hp, LANES), *lam_params,
      subln_w.reshape(1, LANES), *([cache_k] * pages_per_step), *([cache_v] * pages_per_step))
    return out.reshape(b, WIDTH_A)


def _rwkv_prep_kernel(c_ref, p_ref, mu_ref, w0_ref, a0_ref, kk_ref, ka_ref, wdec_ref, waaa_ref, wgate_ref,
                      r_ref, lw_ref, k_ref, v_ref, kkr_ref, a_ref, g_ref):
    c = c_ref[...]
    u = c + mu_ref[...] * (p_ref[...] - c)
    wb = WIDTH_B
    r = u[:, 0:wb]
    kx = u[:, wb:2 * wb]
    v = u[:, 2 * wb:3 * wb]
    o = 3 * wb
    wd = u[:, o:o + LORA_PAD]
    ad = u[:, o + LORA_PAD:o + 2 * LORA_PAD]
    gd = u[:, o + 2 * LORA_PAD:]
    z = w0_ref[...] + _dot(jnp.tanh(wd).astype(BF16), wdec_ref[...])
    nz = -z
    softplus = jnp.maximum(nz, 0.0) + jnp.log(1.0 + jnp.exp(-jnp.abs(nz)))
    w_raw = -softplus - 0.5
    a = _sigmoid(a0_ref[...] + _dot(ad.astype(BF16), waaa_ref[...]))
    r_ref[...] = r
    lw_ref[...] = -jnp.exp(w_raw)
    k_ref[...] = kx * (1.0 + (a - 1.0) * ka_ref[...])
    v_ref[...] = v
    kkr_ref[...] = kx * kk_ref[...]
    a_ref[...] = a
    g_ref[...] = _dot(_sigmoid(gd).astype(BF16), wgate_ref[...])


def rwkv_prep(cols, prev_seq, mu, w0, a0, k_k, k_a, w_decay, w_aaa, w_gate, tm):
    m = cols.shape[0]
    wb = WIDTH_B
    row = lambda n: pl.BlockSpec((1, n), lambda i: (0, 0))
    full = lambda r_, c_: pl.BlockSpec((r_, c_), lambda i: (0, 0))
    tile = lambda n: pl.BlockSpec((tm, n), lambda i: (i, 0))
    return pl.pallas_call(
        _rwkv_prep_kernel,
        grid=(m // tm,),
        in_specs=[tile(RW_PAD_COLS), tile(RW_PAD_COLS), row(RW_PAD_COLS), row(wb), row(wb), row(wb), row(wb),
                  full(LORA_PAD, wb), full(LORA_PAD, wb), full(R_GATE, wb)],
        out_specs=[tile(wb)] * 7,
        out_shape=[jax.ShapeDtypeStruct((m, wb), F32)] * 7,
        compiler_params=_cparams(("parallel",)),
        name="rwkv_prep",
    )(cols, prev_seq, mu, w0, a0, k_k, k_a, w_decay, w_aaa, w_gate)


def _rwkv_chunk_kernel(r_ref, lw_ref, k_ref, v_ref, kkr_ref, a_ref, g_ref, lnw_ref, lnb_ref, rk_ref, s0_ref,
                       y_ref, s_out_ref, s_ref):
    c = pl.program_id(2)
    cl = CHUNK
    n = D_HEAD_B

    @pl.when(c == 0)
    def _():
        s_ref[...] = s0_ref[0, 0]

    def stack(x):
        x2 = jnp.concatenate([x, x], axis=0)
        row = lax.broadcasted_iota(jnp.int32, x2.shape, 0)
        lane = lax.broadcasted_iota(jnp.int32, x2.shape, 1)
        return jnp.where((row // cl) == (lane // n), x2, 0.0)

    def fold(x_bd):
        return x_bd[0:cl, :] + x_bd[cl:2 * cl, :]

    r = r_ref[...]
    lw = lw_ref[...]
    k = k_ref[...]
    v = v_ref[...]
    a = a_ref[...]

    row_c = lax.broadcasted_iota(jnp.int32, lw.shape, 0)
    cum = lw
    shift = 1
    while shift < cl:
        cum = cum + jnp.where(row_c >= shift, pltpu.roll(cum, shift, 0), 0.0)
        shift *= 2
    cum_last = cum[cl - 1:cl, :]

    kk_bd = stack(kkr_ref[...])
    norm = jnp.sqrt(jnp.sum(kk_bd * kk_bd, axis=-1, keepdims=True))
    kk = fold(kk_bd * (1.0 / jnp.maximum(norm, 1e-12)))
    b = kk * a

    e_pos = jnp.exp(cum)
    e_neg = jnp.exp(-cum)
    e_rest = jnp.exp(cum_last - cum)
    r_t = stack(r * e_pos)
    a_t = stack(-kk * jnp.exp(cum - lw))
    b_t = b * e_neg
    k_t = k * e_neg
    b_h = stack(b * e_rest)
    k_h = stack(k * e_rest)
    v_bd = stack(v)
    gamma = jnp.exp(cum_last)

    lhs = jnp.concatenate([a_t, r_t], axis=0).astype(BF16)
    rhs = jnp.concatenate([b_t, b_t, k_t, k_t], axis=0).astype(BF16)
    m1 = _dot_nt(lhs, rhs)
    c2 = 2 * cl
    row = lax.broadcasted_iota(jnp.int32, (c2, c2), 0)
    col = lax.broadcasted_iota(jnp.int32, (c2, c2), 1)
    same_head = (row // cl) == (col // cl)
    strict = jnp.logical_and(same_head, row > col)
    incl = jnp.logical_and(same_head, row >= col)
    a_ab = jnp.where(strict, m1[0:c2, 0:c2], 0.0)
    a_ak = jnp.where(strict, m1[0:c2, c2:2 * c2], 0.0)
    a_rb = jnp.where(incl, m1[c2:2 * c2, 0:c2], 0.0)
    a_rk = jnp.where(incl, m1[c2:2 * c2, c2:2 * c2], 0.0)

    mm = lambda x, y: _dot(x.astype(BF16), y.astype(BF16))
    eye = jnp.where(row == col, 1.0, 0.0)
    diag_blk = (row // INV_BLOCK) == (col // INV_BLOCK)
    x = jnp.where(diag_blk, a_ab, 0.0)
    a_off = a_ab - x
    t_d = eye + x
    span = 2
    while span < INV_BLOCK:
        x = mm(x, x)
        t_d = t_d + mm(t_d, x)
        span *= 2
    nm = mm(t_d, a_off)
    nm_pow = nm
    m_acc = eye + nm
    blocks = 2
    while blocks < cl // INV_BLOCK:
        sq = mm(nm_pow, nm_pow)
        m_acc = m_acc + mm(m_acc, sq)
        nm_pow = sq
        blocks *= 2
    t_inv = mm(m_acc, t_d)

    s_prev = s_ref[...]
    xr = _dot_nt(lhs, s_prev.astype(BF16))
    vv = mm(jnp.concatenate([a_ak, a_rk], axis=0), v_bd)
    u = mm(t_inv, xr[0:c2, :] + vv[0:c2, :])
    y_bd = xr[c2:2 * c2, :] + mm(a_rb, u) + vv[c2:2 * c2, :]
    s_new = s_prev * gamma + _dot_tn(jnp.concatenate([u, v_bd], axis=0).astype(BF16),
                                     jnp.concatenate([b_h, k_h], axis=0).astype(BF16))
    s_ref[...] = s_new

    lane2 = lax.broadcasted_iota(jnp.int32, (c2, LANES), 1)
    row2 = lax.broadcasted_iota(jnp.int32, (c2, LANES), 0)
    head_mask = (row2 // cl) == (lane2 // n)
    mean = jnp.sum(y_bd, axis=-1, keepdims=True) * (1.0 / n)
    d = jnp.where(head_mask, y_bd - mean, 0.0)
    var = jnp.sum(d * d, axis=-1, keepdims=True) * (1.0 / n)
    yn = fold(d * lax.rsqrt(var + GN_EPS))
    yn = yn * lnw_ref[...] + lnb_ref[...]
    rk_sum = jnp.sum(stack(r * k * rk_ref[...]), axis=-1, keepdims=True)
    bonus = fold(rk_sum * v_bd)
    y_ref[...] = ((yn + bonus) * g_ref[...]).astype(y_ref.dtype)

    @pl.when(c == pl.num_programs(2) - 1)
    def _():
        s_out_ref[0, 0] = s_new


def rwkv_chunked(r, lw, k, v, kkr, a, g, lnx_w, lnx_b, r_k, s0_bd, n_seq):
    m = r.shape[0]
    n_pairs = N_HEADS_B // 2
    nc = m // n_seq // CHUNK
    tile = pl.BlockSpec((CHUNK, LANES), lambda s, h, c: (s * nc + c, h))
    par = pl.BlockSpec((1, LANES), lambda s, h, c: (0, h))
    st = pl.BlockSpec((1, 1, LANES, LANES), lambda s, h, c: (s, h, 0, 0))
    return pl.pallas_call(
        _rwkv_chunk_kernel,
        grid=(n_seq, n_pairs, nc),
        in_specs=[tile] * 7 + [par] * 3 + [st],
        out_specs=[tile, st],
        out_shape=[jax.ShapeDtypeStruct((m, WIDTH_B), BF16),
                   jax.ShapeDtypeStruct((n_seq, n_pairs, LANES, LANES), F32)],
        scratch_shapes=[pltpu.VMEM((LANES, LANES), F32)],
        compiler_params=_cparams(("parallel", "parallel", "arbitrary")),
        name="rwkv_chunked",
    )(r, lw, k, v, kkr, a, g, lnx_w.reshape(1, -1), lnx_b.reshape(1, -1), r_k.reshape(1, -1), s0_bd)


def _pair_states_to_bd(s):
    bsz = s.shape[0]
    s = s.reshape(bsz, N_HEADS_B // 2, 2, D_HEAD_B, D_HEAD_B)
    z = jnp.zeros_like(s[:, :, 0])
    top = jnp.concatenate([s[:, :, 0], z], axis=-1)
    bot = jnp.concatenate([z, s[:, :, 1]], axis=-1)
    return jnp.concatenate([top, bot], axis=-2)


def _bd_to_pair_states(s_bd):
    bsz = s_bd.shape[0]
    n = D_HEAD_B
    s0 = s_bd[:, :, 0:n, 0:n]
    s1 = s_bd[:, :, n:2 * n, n:2 * n]
    return jnp.stack([s0, s1], axis=2).reshape(bsz, N_HEADS_B, n, n)


def _merge_kernel(att_ref, rwk_ref, ga_ref, gb_ref, wa_ref, wb_ref, o_ref):
    ya = _dot(att_ref[...], wa_ref[...])
    yb = _dot(rwk_ref[...], wb_ref[...])
    o_ref[...] = (_sigmoid(ga_ref[...]) * ya + _sigmoid(gb_ref[...]) * yb).astype(o_ref.dtype)


def merge_branches(att, rwk, gates, wa, wb, tm, tn):
    m = att.shape[0]
    n = wa.shape[1]
    nb = n // tn
    return pl.pallas_call(
        _merge_kernel,
        grid=(m // tm, nb),
        in_specs=[pl.BlockSpec((tm, WIDTH_A), lambda i, j: (i, 0)),
                  pl.BlockSpec((tm, WIDTH_B), lambda i, j: (i, 0)),
                  pl.BlockSpec((tm, tn), lambda i, j: (i, j)),
                  pl.BlockSpec((tm, tn), lambda i, j: (i, j + nb)),
                  pl.BlockSpec((WIDTH_A, tn), lambda i, j: (0, j)),
                  pl.BlockSpec((WIDTH_B, tn), lambda i, j: (0, j))],
        out_specs=pl.BlockSpec((tm, tn), lambda i, j: (i, j)),
        out_shape=jax.ShapeDtypeStruct((m, n), BF16),
        compiler_params=_cparams(("parallel", "arbitrary")),
        name="merge_branches",
    )(att, rwk, gates, gates, wa, wb)


def _mm_norm_res_kernel(a_ref, w_ref, x_ref, g_ref, o_ref, acc_ref):
    kk = pl.program_id(1)

    @pl.when(kk == 0)
    def _():
        acc_ref[...] = jnp.zeros(acc_ref.shape, F32)

    acc_ref[...] += _dot(a_ref[...], w_ref[...])

    @pl.when(kk == pl.num_programs(1) - 1)
    def _():
        o_ref[...] = x_ref[...] + _rms(acc_ref[...], g_ref[...], NORM_EPS)


def matmul_norm_residual(a, w, x, g, tm, tk, name):
    m, kdim = a.shape
    n = w.shape[1]
    return pl.pallas_call(
        _mm_norm_res_kernel,
        grid=(m // tm, kdim // tk),
        in_specs=[pl.BlockSpec((tm, tk), lambda i, k: (i, k)),
                  pl.BlockSpec((tk, n), lambda i, k: (k, 0)),
                  pl.BlockSpec((tm, n), lambda i, k: (i, 0)),
                  pl.BlockSpec((1, n), lambda i, k: (0, 0))],
        out_specs=pl.BlockSpec((tm, n), lambda i, k: (i, 0)),
        out_shape=jax.ShapeDtypeStruct((m, n), F32),
        scratch_shapes=[pltpu.VMEM((tm, n), F32)],
        compiler_params=_cparams(("parallel", "arbitrary")),
        name=name,
    )(a, w, x, g.reshape(1, n))


def _gate_up_kernel(h_ref, wg_ref, wu_ref, o_ref):
    h = h_ref[...]
    gt = _dot(h, wg_ref[...])
    up = _dot(h, wu_ref[...])
    o_ref[...] = (gt * _sigmoid(gt) * up).astype(o_ref.dtype)


def ffn_gate_up(h, wg, wu, tm, tn):
    m, kdim = h.shape
    n = wg.shape[1]
    return pl.pallas_call(
        _gate_up_kernel,
        grid=(m // tm, n // tn),
        in_specs=[pl.BlockSpec((tm, kdim), lambda i, j: (i, 0)),
                  pl.BlockSpec((kdim, tn), lambda i, j: (0, j)),
                  pl.BlockSpec((kdim, tn), lambda i, j: (0, j))],
        out_specs=pl.BlockSpec((tm, tn), lambda i, j: (i, j)),
        out_shape=jax.ShapeDtypeStruct((m, n), BF16),
        compiler_params=_cparams(("parallel", "arbitrary")),
        name="ffn_gate_up",
    )(h, wg, wu)


def _pad_lora_cols(x):
    o = 3 * WIDTH_B
    z = jnp.zeros(x.shape[:-1] + (LORA_PAD - R_DECAY,), x.dtype)
    return jnp.concatenate([x[..., :o], x[..., o:o + R_DECAY], z,
                            x[..., o + R_DECAY:o + R_DECAY + R_AAA], z,
                            x[..., o + R_DECAY + R_AAA:]], axis=-1)


def _unpad_lora_cols(x):
    o = 3 * WIDTH_B
    return jnp.concatenate([x[..., :o], x[..., o:o + R_DECAY], x[..., o + LORA_PAD:o + LORA_PAD + R_AAA],
                            x[..., o + 2 * LORA_PAD:]], axis=-1)


def _pad_rows(w, rows):
    return jnp.concatenate([w, jnp.zeros((rows - w.shape[0],) + w.shape[1:], w.dtype)], axis=0)


def _group(x, pos, prev_rows, s0, n_seq, wts, tiles, q_dtype, attention_fn):
    tm = tiles["tm"]
    h = rmsnorm_bf16(x, wts["norm_mix_pre"], tm)
    tables = rope_tables(pos)
    (q,) = projection(h, wts["w_q"], [q_dtype], tm, tiles["tn"], tables, D_HEAD_A ** -0.5, name="proj_q")
    k, k16 = projection(h, wts["w_k"], [F32, BF16], tm, tiles["tn"], tables, name="proj_k")
    v, v16 = projection(h, wts["w_v"], [F32, BF16], tm, tiles["tn"], name="proj_v")
    (rw,) = projection(h, wts["w_rw"], [F32], tm, tiles["tn"], name="proj_rw")
    (gates,) = projection(h, wts["w_gates"], [F32], tm, tiles["tn"], name="proj_gates")

    att = attention_fn(q, k, v, k16, v16)

    prev_seq = prev_rows(rw)
    prep = rwkv_prep(rw, prev_seq, wts["mu"], wts["w0"], wts["a0"], wts["k_k"], wts["k_a"],
                     wts["w_decay"], wts["w_aaa"], wts["w_gate"], tiles["tm_prep"])
    m = x.shape[0]
    t_seq = m // n_seq
    if t_seq % CHUNK:
        pad = CHUNK - t_seq % CHUNK
        prep = [jnp.pad(p.reshape(n_seq, t_seq, WIDTH_B), ((0, 0), (0, pad), (0, 0))).reshape(-1, WIDTH_B)
                for p in prep]
    rwk, s_bd = rwkv_chunked(*prep, wts["lnx_w"], wts["lnx_b"], wts["r_k"], _pair_states_to_bd(s0), n_seq)
    if t_seq % CHUNK:
        rwk = rwk.reshape(n_seq, -1, WIDTH_B)[:, :t_seq].reshape(m, WIDTH_B)
    s_new = _bd_to_pair_states(s_bd)

    mrg = merge_branches(att, rwk, gates, wts["w_branch_a"], wts["w_branch_b"], tm, tiles["tn"])
    x1 = matmul_norm_residual(mrg, wts["w_out"], x, wts["norm_mix_post"], tiles["tm_out"], tiles["tk_out"], "out_proj")
    hf = rmsnorm_bf16(x1, wts["norm_ffn_pre"], tm)
    act = ffn_gate_up(hf, wts["w_ffn_gate"], wts["w_ffn_up"], tm, tiles["tn_ffn"])
    y = matmul_norm_residual(act, wts["w_ffn_down"], x1, wts["norm_ffn_post"], tiles["tm_out"], tiles["tk_ffn"], "ffn_down")
    return y, k, v, rw, s_new


def kernel(x_prompt, x_sample, cache_k, cache_v, state_rwkv, state_shift, page_table, norm_mix_pre, w_in, lambda_q1, lambda_k1, lambda_q2, lambda_k2, subln_w, rw_mu, w0, w_decay, a0, w_aaa, w_gate_lora, k_k, k_a, r_k, lnx_w, lnx_b, w_branch_a, w_branch_b, w_out, norm_mix_post, norm_ffn_pre, w_ffn_gate, w_ffn_up, w_ffn_down, norm_ffn_post):
    depth = w_in.shape[0]
    bsz, seq = x_prompt.shape[0], x_prompt.shape[1]
    db, ds = x_sample.shape[0], x_sample.shape[1]
    assert depth == 1 and bsz == 1 and ds == 1
    n_pool = cache_k.shape[1]
    n_pages = page_table.shape[1]
    past = n_pages * PAGE_SIZE
    l = 0
    lambda_init = 0.8 - 0.6 * math.exp(-0.3 * l)

    wi = w_in[l]
    row = lambda p: p.reshape(1, -1)
    wts = {
        "norm_mix_pre": norm_mix_pre[l],
        "w_q": wi[:, :WIDTH_A].astype(BF16),
        "w_k": wi[:, WIDTH_A:2 * WIDTH_A].astype(BF16),
        "w_v": wi[:, 2 * WIDTH_A:A_COLS].astype(BF16),
        "w_rw": _pad_lora_cols(wi[:, A_COLS:A_COLS + RW_COLS]).astype(BF16),
        "w_gates": wi[:, A_COLS + RW_COLS:].astype(BF16),
        "mu": _pad_lora_cols(row(rw_mu[l])),
        "w0": row(w0[l]), "a0": row(a0[l]), "k_k": row(k_k[l]), "k_a": row(k_a[l]),
        "w_decay": _pad_rows(w_decay[l], LORA_PAD).astype(BF16),
        "w_aaa": _pad_rows(w_aaa[l], LORA_PAD).astype(BF16),
        "w_gate": w_gate_lora[l].astype(BF16),
        "lnx_w": lnx_w[l], "lnx_b": lnx_b[l], "r_k": r_k[l],
        "w_branch_a": w_branch_a[l].astype(BF16), "w_branch_b": w_branch_b[l].astype(BF16),
        "w_out": w_out[l].astype(BF16), "norm_mix_post": norm_mix_post[l],
        "norm_ffn_pre": norm_ffn_pre[l],
        "w_ffn_gate": w_ffn_gate[l].astype(BF16), "w_ffn_up": w_ffn_up[l].astype(BF16),
        "w_ffn_down": w_ffn_down[l].astype(BF16), "norm_ffn_post": norm_ffn_post[l],
    }
    lam_params = [row(p[l]) for p in (lambda_q1, lambda_k1, lambda_q2, lambda_k2)]
    subln = subln_w[l]

    tiles_p = dict(tm=512, tn=512, tm_prep=256, tm_out=512, tk_out=512, tn_ffn=512, tk_ffn=512)

    def prompt_attention(q, k, v, k16, v16):
        del k, v
        return prompt_diff_attention(q, k16, v16, lam_params, subln, lambda_init, 512, 512)

    def prompt_prev(rw):
        return jnp.concatenate([jnp.zeros((1, RW_PAD_COLS), rw.dtype), rw[:-1]], axis=0)

    s0_p = jnp.zeros((bsz, N_HEADS_B, D_HEAD_B, D_HEAD_B), F32)
    y_p, k_p, v_p, rw_p, s_p = _group(x_prompt.reshape(seq, D_MODEL), jnp.arange(seq), prompt_prev, s0_p, 1,
                                      wts, tiles_p, BF16, prompt_attention)

    tiles_s = dict(tm=db, tn=512, tm_prep=db, tm_out=db, tk_out=512, tn_ffn=512, tk_ffn=512)
    ck = jnp.transpose(cache_k[l], (0, 2, 3, 1)).reshape(n_pool, WIDTH_A, PAGE_SIZE)
    cv = cache_v[l]

    def sample_attention(q, k, v, k16, v16):
        del k16, v16
        return sample_diff_attention(q, k, v, ck, cv, page_table, lam_params, subln, lambda_init, 4)

    def sample_prev(rw):
        del rw
        return _pad_lora_cols(state_shift[l])

    y_s, k_s, v_s, rw_s, s_s = _group(x_sample.reshape(db, D_MODEL), jnp.full((db,), past, jnp.int32), sample_prev,
                                      state_rwkv[l], db, wts, tiles_s, F32, sample_attention)

    return (y_p.reshape(bsz, seq, D_MODEL),
            y_s.reshape(db, ds, D_MODEL),
            k_p.reshape(1, bsz, seq, 2 * N_HEADS_A, D_HEAD_A),
            v_p.reshape(1, bsz, seq, N_HEADS_A, 2 * D_HEAD_A),
            s_p.reshape(1, bsz, N_HEADS_B, D_HEAD_B, D_HEAD_B),
            _unpad_lora_cols(rw_p[-1:]).reshape(1, bsz, RW_COLS),
            k_s.reshape(1, db, ds, 2 * N_HEADS_A, D_HEAD_A),
            v_s.reshape(1, db, ds, N_HEADS_A, 2 * D_HEAD_A),
            s_s.reshape(1, db, N_HEADS_B, D_HEAD_B, D_HEAD_B),
            _unpad_lora_cols(rw_s).reshape(1, db, RW_COLS))
```

```python
import functools
import math

import jax
import jax.numpy as jnp
from jax import lax
from jax.experimental import pallas as pl
from jax.experimental.pallas import tpu as pltpu

F32 = jnp.float32
BF16 = jnp.bfloat16

D_MODEL = 2048
N_HEADS_A = 8
D_HEAD_A = 64
WIDTH_A = 2 * N_HEADS_A * D_HEAD_A
N_HEADS_B = 16
D_HEAD_B = 64
WIDTH_B = N_HEADS_B * D_HEAD_B
R_DECAY = 96
R_AAA = 96
R_GATE = 256
ROPE_THETA = 10000.0
NORM_EPS = 1e-6
SUBLN_EPS = 1e-5
GN_EPS = 64e-5
PAGE_SIZE = 128
A_COLS = 3 * WIDTH_A
RW_COLS = 3 * WIDTH_B + R_DECAY + R_AAA + R_GATE

LANES = 128
LORA_PAD = 128
RW_PAD_COLS = 3 * WIDTH_B + 2 * LORA_PAD + R_GATE
CHUNK = 64
INV_BLOCK = 16
VMEM_LIMIT = 56 * 1024 * 1024


def _cparams(sem):
    return pltpu.CompilerParams(dimension_semantics=sem, vmem_limit_bytes=VMEM_LIMIT)


def _dot(a, b):
    return jnp.dot(a, b, preferred_element_type=F32)


def _dot_nt(a, b):
    return lax.dot_general(a, b, (((1,), (1,)), ((), ())), preferred_element_type=F32)


def _dot_tn(a, b):
    return lax.dot_general(a, b, (((0,), (0,)), ((), ())), preferred_element_type=F32)


def _sigmoid(x):
    return 1.0 / (1.0 + jnp.exp(-x))


def _rms(x, g, eps):
    return x * lax.rsqrt(jnp.mean(x * x, axis=-1, keepdims=True) + eps) * g


def _norm_kernel(x_ref, g_ref, o_ref):
    o_ref[...] = _rms(x_ref[...], g_ref[...], NORM_EPS).astype(o_ref.dtype)


def rmsnorm_bf16(x, g, tm):
    m, d = x.shape
    return pl.pallas_call(
        _norm_kernel,
        grid=(m // tm,),
        in_specs=[pl.BlockSpec((tm, d), lambda i: (i, 0)),
                  pl.BlockSpec((1, d), lambda i: (0, 0))],
        out_specs=pl.BlockSpec((tm, d), lambda i: (i, 0)),
        out_shape=jax.ShapeDtypeStruct((m, d), BF16),
        compiler_params=_cparams(("parallel",)),
        name="rmsnorm_bf16",
    )(x, g.reshape(1, d))


def _rope_tile(x, cos, sin_signed):
    lane = lax.broadcasted_iota(jnp.int32, x.shape, 1)
    first_half = (lane % D_HEAD_A) < (D_HEAD_A // 2)
    partner = jnp.where(first_half,
                        pltpu.roll(x, LANES - D_HEAD_A // 2, 1),
                        pltpu.roll(x, D_HEAD_A // 2, 1))
    return x * cos + partner * sin_signed


def _proj_kernel(*refs, rope, scale, n_out):
    if rope:
        a_ref, w_ref, cos_ref, sin_ref = refs[:4]
        outs = refs[4:]
    else:
        a_ref, w_ref = refs[:2]
        outs = refs[2:]
    acc = _dot(a_ref[...], w_ref[...])
    if rope:
        cos = cos_ref[...]
        sin = sin_ref[...]
        tn = acc.shape[1]
        acc = jnp.concatenate(
            [_rope_tile(acc[:, c * LANES:(c + 1) * LANES], cos, sin) for c in range(tn // LANES)], axis=1)
    if scale != 1.0:
        acc = acc * scale
    for o_ref in outs[:n_out]:
        o_ref[...] = acc.astype(o_ref.dtype)


def projection(a, w, out_dtypes, tm, tn, rope_tables=None, scale=1.0, name="projection"):
    m, kdim = a.shape
    n = w.shape[1]
    in_specs = [pl.BlockSpec((tm, kdim), lambda i, j: (i, 0)),
                pl.BlockSpec((kdim, tn), lambda i, j: (0, j))]
    args = [a, w]
    if rope_tables is not None:
        in_specs += [pl.BlockSpec((tm, LANES), lambda i, j: (i, 0))] * 2
        args += list(rope_tables)
    outs = pl.pallas_call(
        functools.partial(_proj_kernel, rope=rope_tables is not None, scale=scale, n_out=len(out_dtypes)),
        grid=(m // tm, n // tn),
        in_specs=in_specs,
        out_specs=[pl.BlockSpec((tm, tn), lambda i, j: (i, j)) for _ in out_dtypes],
        out_shape=[jax.ShapeDtypeStruct((m, n), dt) for dt in out_dtypes],
        compiler_params=_cparams(("parallel", "arbitrary")),
        name=name,
    )(*args)
    return outs


def rope_tables(pos):
    half = D_HEAD_A // 2
    inv = 1.0 / (ROPE_THETA ** (jnp.arange(half, dtype=F32) / half))
    ang = pos.astype(F32)[:, None] * inv[None, :]
    cos = jnp.cos(ang)
    sin = jnp.sin(ang)
    cos2 = jnp.concatenate([cos, cos, cos, cos], axis=1)
    sin2 = jnp.concatenate([-sin, sin, -sin, sin], axis=1)
    return cos2, sin2


def _diff_lambda(lq1, lk1, lq2, lk2, lambda_init):
    s1 = jnp.sum(lq1 * lk1, axis=-1, keepdims=True)
    s2 = jnp.sum(lq2 * lk2, axis=-1, keepdims=True)
    return jnp.exp(s1) - jnp.exp(s2) + lambda_init


def _flash_kernel(q_ref, k_ref, v_ref, lq1_ref, lk1_ref, lq2_ref, lk2_ref, g_ref, o_ref,
                  qs_ref, m_ref, l_ref, acc_ref, *, tq, tk, lambda_init):
    qi = pl.program_id(1)
    ratio = tq // tk
    qt = jnp.transpose(q_ref[...].astype(F32))
    qt2 = jnp.concatenate([qt, qt], axis=1)
    row = lax.broadcasted_iota(jnp.int32, qt2.shape, 0)
    col = lax.broadcasted_iota(jnp.int32, qt2.shape, 1)
    qs_ref[...] = jnp.where((row < D_HEAD_A) == (col < tq), qt2, 0.0).astype(BF16)
    m_ref[...] = jnp.full(m_ref.shape, -jnp.inf, F32)
    l_ref[...] = jnp.zeros(l_ref.shape, F32)
    acc_ref[...] = jnp.zeros(acc_ref.shape, F32)

    def block(kv, masked):
        start = pl.multiple_of(kv * tk, tk)
        s = _dot(k_ref[pl.ds(start, tk), :], qs_ref[...])
        if masked:
            krow = lax.broadcasted_iota(jnp.int32, s.shape, 0)
            qcol = lax.broadcasted_iota(jnp.int32, s.shape, 1)
            q_pos = qi * tq + jnp.where(qcol >= tq, qcol - tq, qcol)
            s = jnp.where(kv * tk + krow <= q_pos, s, -jnp.inf)
        m_prev = m_ref[...]
        m_new = jnp.maximum(m_prev, jnp.max(s, axis=0, keepdims=True))
        alpha = jnp.exp(m_prev - m_new)
        p = jnp.exp(s - m_new)
        l_ref[...] = alpha * l_ref[...] + jnp.sum(p, axis=0, keepdims=True)
        acc_ref[...] = alpha * acc_ref[...] + _dot_tn(v_ref[pl.ds(start, tk), :], p.astype(BF16))
        m_ref[...] = m_new

    def full_block(kv, carry):
        block(kv, False)
        return carry

    lax.fori_loop(0, qi * ratio, full_block, 0)
    for r in range(ratio):
        block(qi * ratio + r, True)

    lam = _diff_lambda(lq1_ref[...], lk1_ref[...], lq2_ref[...], lk2_ref[...], lambda_init)
    o = acc_ref[...] / l_ref[...]
    o = jnp.transpose(o[:, 0:tq] - lam * o[:, tq:2 * tq])
    o = _rms(o, g_ref[...], SUBLN_EPS) * (1.0 - lambda_init)
    o_ref[...] = o.astype(o_ref.dtype)


def prompt_diff_attention(q, k, v, lam_params, subln_w, lambda_init, tq, tk):
    t = q.shape[0]
    n_pairs = N_HEADS_A
    small = pl.BlockSpec((1, D_HEAD_A), lambda h, i: (0, 0))
    whole = pl.BlockSpec((t, LANES), lambda h, i: (0, h))
    return pl.pallas_call(
        functools.partial(_flash_kernel, tq=tq, tk=tk, lambda_init=lambda_init),
        grid=(n_pairs, t // tq),
        in_specs=[pl.BlockSpec((tq, LANES), lambda h, i: (i, h)),
                  whole, whole,
                  small, small, small, small,
                  pl.BlockSpec((1, LANES), lambda h, i: (0, 0))],
        out_specs=pl.BlockSpec((tq, LANES), lambda h, i: (i, h)),
        out_shape=jax.ShapeDtypeStruct((t, WIDTH_A), BF16),
        scratch_shapes=[pltpu.VMEM((LANES, 2 * tq), BF16),
                        pltpu.VMEM((1, 2 * tq), F32),
                        pltpu.VMEM((1, 2 * tq), F32),
                        pltpu.VMEM((LANES, 2 * tq), F32)],
        compiler_params=_cparams(("parallel", "arbitrary")),
        name="prompt_diff_attention",
    )(q, k, v, *lam_params, subln_w.reshape(1, LANES))


def _decode_kernel(pt_ref, q_ref, kn_ref, vn_ref, lq1_ref, lk1_ref, lq2_ref, lk2_ref, g_ref, *rest,
                   pages_per_step, lambda_init):
    del pt_ref
    k_refs = rest[:pages_per_step]
    v_refs = rest[pages_per_step:2 * pages_per_step]
    o_ref = rest[2 * pages_per_step]
    qm_ref, ex_ref, m_ref, l_ref, acc_ref = rest[2 * pages_per_step + 1:]
    hp = N_HEADS_A
    n_rows = 2 * hp
    j = pl.program_id(1)

    def pair_mask():
        row = lax.broadcasted_iota(jnp.int32, (n_rows, PAGE_SIZE * hp), 0)
        lane = lax.broadcasted_iota(jnp.int32, (n_rows, PAGE_SIZE * hp), 1)
        return (lane % hp) == (row % hp)

    @pl.when(j == 0)
    def _():
        row = lax.broadcasted_iota(jnp.int32, (n_rows, WIDTH_A), 0)
        lane = lax.broadcasted_iota(jnp.int32, (n_rows, WIDTH_A), 1)
        qb = jnp.broadcast_to(q_ref[0], (n_rows, WIDTH_A))
        qm_ref[...] = jnp.where(lane // D_HEAD_A == 2 * (row % hp) + row // hp, qb, 0.0)
        tok = lax.broadcasted_iota(jnp.int32, ex_ref.shape, 0)
        pos = lax.broadcasted_iota(jnp.int32, ex_ref.shape, 1)
        ex_ref[...] = jnp.where(pos // hp == tok, 1.0, 0.0).astype(BF16)
        m_ref[...] = jnp.full(m_ref.shape, -jnp.inf, F32)
        l_ref[...] = jnp.zeros(l_ref.shape, F32)
        acc_ref[...] = jnp.zeros(acc_ref.shape, F32)

    qm = qm_ref[...].astype(BF16)
    keep = pair_mask()
    pages = range(pages_per_step)
    s = jnp.concatenate([_dot(qm, k_refs[i][0].astype(BF16)) for i in pages], axis=1)
    m_prev = m_ref[...]
    m_new = jnp.maximum(m_prev, jnp.max(s, axis=-1, keepdims=True))
    alpha = jnp.exp(m_prev - m_new)
    p = jnp.exp(s - m_new)
    l_ref[...] = alpha * l_ref[...] + jnp.sum(p, axis=-1, keepdims=True)
    p16 = p.astype(BF16)
    p_wide = [jnp.where(keep, _dot(p16[:, i * PAGE_SIZE:(i + 1) * PAGE_SIZE], ex_ref[...]), 0.0).astype(BF16)
              for i in pages]
    pv = _dot(p_wide[0], v_refs[0][0].reshape(PAGE_SIZE * hp, LANES).astype(BF16))
    for i in pages[1:]:
        pv += _dot(p_wide[i], v_refs[i][0].reshape(PAGE_SIZE * hp, LANES).astype(BF16))
    acc_ref[...] = alpha * acc_ref[...] + pv
    m_ref[...] = m_new

    @pl.when(j == pl.num_programs(1) - 1)
    def _():
        s = jnp.sum(qm_ref[...] * kn_ref[0], axis=-1, keepdims=True)
        m_prev = m_ref[...]
        m_new = jnp.maximum(m_prev, s)
        alpha = jnp.exp(m_prev - m_new)
        p = jnp.exp(s - m_new)
        l = alpha * l_ref[...] + p
        vn = vn_ref[0]
        acc = alpha * acc_ref[...] + p * jnp.concatenate([vn, vn], axis=0)
        lam = _diff_lambda(lq1_ref[...], lk1_ref[...], lq2_ref[...], lk2_ref[...], lambda_init)
        o = acc / l
        o = o[0:hp, :] - lam * o[hp:n_rows, :]
        o_ref[0] = (_rms(o, g_ref[...], SUBLN_EPS) * (1.0 - lambda_init)).astype(o_ref.dtype)


def sample_diff_attention(q, k_new, v_new, cache_k, cache_v, page_table, lam_params, subln_w,
                          lambda_init, pages_per_step):
    b = q.shape[0]
    n_pages = page_table.shape[1]
    steps = n_pages // pages_per_step
    hp = N_HEADS_A
    row3 = lambda x: x.reshape(b, 1, WIDTH_A)
    vec = pl.BlockSpec((1, 1, WIDTH_A), lambda bi, j, pt: (bi, 0, 0))
    small = pl.BlockSpec((1, D_HEAD_A), lambda bi, j, pt: (0, 0))
    pairs = pl.BlockSpec((1, hp, LANES), lambda bi, j, pt: (bi, 0, 0))

    def page_spec(i, shape):
        zeros = (0,) * len(shape)
        return pl.BlockSpec((1,) + shape,
                            lambda bi, j, pt: (pt[bi * n_pages + j * pages_per_step + i],) + zeros)

    grid_spec = pltpu.PrefetchScalarGridSpec(
        num_scalar_prefetch=1,
        grid=(b, steps),
        in_specs=[vec, vec, pairs, small, small, small, small,
                  pl.BlockSpec((1, LANES), lambda bi, j, pt: (0, 0))]
                 + [page_spec(i, (WIDTH_A, PAGE_SIZE)) for i in range(pages_per_step)]
                 + [page_spec(i, (PAGE_SIZE, hp, LANES)) for i in range(pages_per_step)],
        out_specs=pairs,
        scratch_shapes=[pltpu.VMEM((2 * hp, WIDTH_A), F32),
                        pltpu.VMEM((PAGE_SIZE, PAGE_SIZE * hp), BF16),
                        pltpu.VMEM((2 * hp, 1), F32),
                        pltpu.VMEM((2 * hp, 1), F32),
                        pltpu.VMEM((2 * hp, LANES), F32)],
    )
    out = pl.pallas_call(
        functools.partial(_decode_kernel, pages_per_step=pages_per_step, lambda_init=lambda_init),
        grid_spec=grid_spec,
        out_shape=jax.ShapeDtypeStruct((b, hp, LANES), BF16),
        compiler_params=_cparams(("parallel", "arbitrary")),
        name="sample_diff_attention",
    )(page_table.reshape(-1), row3(q), row3(k_new), v_new.reshape(b, hp, LANES), *lam_params,
      subln_w.reshape(1, LANES), *([cache_k] * pages_per_step), *([cache_v] * pages_per_step))
    return out.reshape(b, WIDTH_A)


def _rwkv_prep_kernel(c_ref, p_ref, mu_ref, w0_ref, a0_ref, kk_ref, ka_ref, wdec_ref, waaa_ref, wgate_ref,
                      r_ref, lw_ref, k_ref, v_ref, kkr_ref, a_ref, g_ref):
    c = c_ref[...]
    u = c + mu_ref[...] * (p_ref[...] - c)
    wb = WIDTH_B
    r = u[:, 0:wb]
    kx = u[:, wb:2 * wb]
    v = u[:, 2 * wb:3 * wb]
    o = 3 * wb
    wd = u[:, o:o + LORA_PAD]
    ad = u[:, o + LORA_PAD:o + 2 * LORA_PAD]
    gd = u[:, o + 2 * LORA_PAD:]
    z = w0_ref[...] + _dot(jnp.tanh(wd).astype(BF16), wdec_ref[...])
    nz = -z
    softplus = jnp.maximum(nz, 0.0) + jnp.log(1.0 + jnp.exp(-jnp.abs(nz)))
    w_raw = -softplus - 0.5
    a = _sigmoid(a0_ref[...] + _dot(ad.astype(BF16), waaa_ref[...]))
    r_ref[...] = r
    lw_ref[...] = -jnp.exp(w_raw)
    k_ref[...] = kx * (1.0 + (a - 1.0) * ka_ref[...])
    v_ref[...] = v
    kkr_ref[...] = kx * kk_ref[...]
    a_ref[...] = a
    g_ref[...] = _dot(_sigmoid(gd).astype(BF16), wgate_ref[...])


def rwkv_prep(cols, prev_seq, mu, w0, a0, k_k, k_a, w_decay, w_aaa, w_gate, tm):
    m = cols.shape[0]
    wb = WIDTH_B
    row = lambda n: pl.BlockSpec((1, n), lambda i: (0, 0))
    full = lambda r_, c_: pl.BlockSpec((r_, c_), lambda i: (0, 0))
    tile = lambda n: pl.BlockSpec((tm, n), lambda i: (i, 0))
    return pl.pallas_call(
        _rwkv_prep_kernel,
        grid=(m // tm,),
        in_specs=[tile(RW_PAD_COLS), tile(RW_PAD_COLS), row(RW_PAD_COLS), row(wb), row(wb), row(wb), row(wb),
                  full(LORA_PAD, wb), full(LORA_PAD, wb), full(R_GATE, wb)],
        out_specs=[tile(wb)] * 7,
        out_shape=[jax.ShapeDtypeStruct((m, wb), F32)] * 7,
        compiler_params=_cparams(("parallel",)),
        name="rwkv_prep",
    )(cols, prev_seq, mu, w0, a0, k_k, k_a, w_decay, w_aaa, w_gate)


def _rwkv_pair_chunk(r, lw, k, v, kkr, a, g, lnw, lnb, rk, s_prev):
    cl = CHUNK
    n = D_HEAD_B

    def stack(x):
        x2 = jnp.concatenate([x, x], axis=0)
        row = lax.broadcasted_iota(jnp.int32, x2.shape, 0)
        lane = lax.broadcasted_iota(jnp.int32, x2.shape, 1)
        return jnp.where((row // cl) == (lane // n), x2, 0.0)

    def fold(x_bd):
        return x_bd[0:cl, :] + x_bd[cl:2 * cl, :]

    row_c = lax.broadcasted_iota(jnp.int32, lw.shape, 0)
    cum = lw
    shift = 1
    while shift < cl:
        cum = cum + jnp.where(row_c >= shift, pltpu.roll(cum, shift, 0), 0.0)
        shift *= 2
    cum_last = cum[cl - 1:cl, :]

    kk_bd = stack(kkr)
    norm = jnp.sqrt(jnp.sum(kk_bd * kk_bd, axis=-1, keepdims=True))
    kk = fold(kk_bd * (1.0 / jnp.maximum(norm, 1e-12)))
    b = kk * a

    e_pos = jnp.exp(cum)
    e_neg = jnp.exp(-cum)
    e_rest = jnp.exp(cum_last - cum)
    r_t = stack(r * e_pos)
    a_t = stack(-kk * jnp.exp(cum - lw))
    b_t = b * e_neg
    k_t = k * e_neg
    b_h = stack(b * e_rest)
    k_h = stack(k * e_rest)
    v_bd = stack(v)
    gamma = jnp.exp(cum_last)

    lhs = jnp.concatenate([a_t, r_t], axis=0).astype(BF16)
    rhs = jnp.concatenate([b_t, b_t, k_t, k_t], axis=0).astype(BF16)
    m1 = _dot_nt(lhs, rhs)
    xr = _dot_nt(lhs, s_prev.astype(BF16))
    yield
    c2 = 2 * cl
    row = lax.broadcasted_iota(jnp.int32, (c2, c2), 0)
    col = lax.broadcasted_iota(jnp.int32, (c2, c2), 1)
    same_head = (row // cl) == (col // cl)
    strict = jnp.logical_and(same_head, row > col)
    incl = jnp.logical_and(same_head, row >= col)
    a_ab = jnp.where(strict, m1[0:c2, 0:c2], 0.0)
    a_ak = jnp.where(strict, m1[0:c2, c2:2 * c2], 0.0)
    a_rb = jnp.where(incl, m1[c2:2 * c2, 0:c2], 0.0)
    a_rk = jnp.where(incl, m1[c2:2 * c2, c2:2 * c2], 0.0)

    mm = lambda x, y: _dot(x.astype(BF16), y.astype(BF16))
    eye = jnp.where(row == col, 1.0, 0.0)
    diag_blk = (row // INV_BLOCK) == (col // INV_BLOCK)
    x = jnp.where(diag_blk, a_ab, 0.0)
    a_off = a_ab - x
    t_d = eye + x
    vv = mm(jnp.concatenate([a_ak, a_rk], axis=0), v_bd)
    span = 2
    while span < INV_BLOCK:
        x = mm(x, x)
        yield
        t_d = t_d + mm(t_d, x)
        span *= 2
    yield
    nm = mm(t_d, a_off)
    yield
    nm_pow = nm
    m_acc = eye + nm
    blocks = 2
    while blocks < cl // INV_BLOCK:
        sq = mm(nm_pow, nm_pow)
        yield
        m_acc = m_acc + mm(m_acc, sq)
        yield
        nm_pow = sq
        blocks *= 2
    t_inv = mm(m_acc, t_d)
    yield
    u = mm(t_inv, xr[0:c2, :] + vv[0:c2, :])
    yield
    y_bd = xr[c2:2 * c2, :] + mm(a_rb, u) + vv[c2:2 * c2, :]
    s_new = s_prev * gamma + _dot_tn(jnp.concatenate([u, v_bd], axis=0).astype(BF16),
                                     jnp.concatenate([b_h, k_h], axis=0).astype(BF16))
    yield

    lane2 = lax.broadcasted_iota(jnp.int32, (c2, LANES), 1)
    row2 = lax.broadcasted_iota(jnp.int32, (c2, LANES), 0)
    head_mask = (row2 // cl) == (lane2 // n)
    mean = jnp.sum(y_bd, axis=-1, keepdims=True) * (1.0 / n)
    d = jnp.where(head_mask, y_bd - mean, 0.0)
    var = jnp.sum(d * d, axis=-1, keepdims=True) * (1.0 / n)
    yn = fold(d * lax.rsqrt(var + GN_EPS))
    yn = yn * lnw + lnb
    rk_sum = jnp.sum(stack(r * k * rk), axis=-1, keepdims=True)
    bonus = fold(rk_sum * v_bd)
    return (yn + bonus) * g, s_new


def _run_interleaved(generators):
    results = [None] * len(generators)
    active = list(enumerate(generators))
    while active:
        still = []
        for i, gen in active:
            try:
                next(gen)
                still.append((i, gen))
            except StopIteration as done:
                results[i] = done.value
        active = still
    return results


def _rwkv_chunk_kernel(r_ref, lw_ref, k_ref, v_ref, kkr_ref, a_ref, g_ref, lnw_ref, lnb_ref, rk_ref, s0_ref,
                       y_ref, s_out_ref, s_ref, *, pairs_per_step):
    c = pl.program_id(2)

    @pl.when(c == 0)
    def _():
        s_ref[...] = s0_ref[0]

    in_refs = (r_ref, lw_ref, k_ref, v_ref, kkr_ref, a_ref, g_ref, lnw_ref, lnb_ref, rk_ref)
    lanes = [slice(p * LANES, (p + 1) * LANES) for p in range(pairs_per_step)]
    loaded = [tuple(ref[:, ln] for ref in in_refs) + (s_ref[p],) for p, ln in enumerate(lanes)]
    results = _run_interleaved([_rwkv_pair_chunk(*args) for args in loaded])
    for p, (y, s_new) in enumerate(results):
        y_ref[:, lanes[p]] = y.astype(y_ref.dtype)
        s_ref[p] = s_new

    @pl.when(c == pl.num_programs(2) - 1)
    def _():
        s_out_ref[0] = s_ref[...]


def rwkv_chunked(r, lw, k, v, kkr, a, g, lnx_w, lnx_b, r_k, s0_bd, n_seq, pairs_per_step):
    m = r.shape[0]
    n_pairs = N_HEADS_B // 2
    nc = m // n_seq // CHUNK
    pps = pairs_per_step
    tile = pl.BlockSpec((CHUNK, pps * LANES), lambda s, h, c: (s * nc + c, h))
    par = pl.BlockSpec((1, pps * LANES), lambda s, h, c: (0, h))
    st = pl.BlockSpec((1, pps, LANES, LANES), lambda s, h, c: (s, h, 0, 0))
    return pl.pallas_call(
        functools.partial(_rwkv_chunk_kernel, pairs_per_step=pps),
        grid=(n_seq, n_pairs // pps, nc),
        in_specs=[tile] * 7 + [par] * 3 + [st],
        out_specs=[tile, st],
        out_shape=[jax.ShapeDtypeStruct((m, WIDTH_B), BF16),
                   jax.ShapeDtypeStruct((n_seq, n_pairs, LANES, LANES), F32)],
        scratch_shapes=[pltpu.VMEM((pps, LANES, LANES), F32)],
        compiler_params=_cparams(("parallel", "parallel", "arbitrary")),
        name="rwkv_chunked",
    )(r, lw, k, v, kkr, a, g, lnx_w.reshape(1, -1), lnx_b.reshape(1, -1), r_k.reshape(1, -1), s0_bd)


def _pair_states_to_bd(s):
    bsz = s.shape[0]
    s = s.reshape(bsz, N_HEADS_B // 2, 2, D_HEAD_B, D_HEAD_B)
    z = jnp.zeros_like(s[:, :, 0])
    top = jnp.concatenate([s[:, :, 0], z], axis=-1)
    bot = jnp.concatenate([z, s[:, :, 1]], axis=-1)
    return jnp.concatenate([top, bot], axis=-2)


def _bd_to_pair_states(s_bd):
    bsz = s_bd.shape[0]
    n = D_HEAD_B
    s0 = s_bd[:, :, 0:n, 0:n]
    s1 = s_bd[:, :, n:2 * n, n:2 * n]
    return jnp.stack([s0, s1], axis=2).reshape(bsz, N_HEADS_B, n, n)


def _merge_kernel(att_ref, rwk_ref, ga_ref, gb_ref, wa_ref, wb_ref, o_ref):
    ya = _dot(att_ref[...], wa_ref[...])
    yb = _dot(rwk_ref[...], wb_ref[...])
    o_ref[...] = (_sigmoid(ga_ref[...]) * ya + _sigmoid(gb_ref[...]) * yb).astype(o_ref.dtype)


def merge_branches(att, rwk, gates, wa, wb, tm, tn):
    m = att.shape[0]
    n = wa.shape[1]
    nb = n // tn
    return pl.pallas_call(
        _merge_kernel,
        grid=(m // tm, nb),
        in_specs=[pl.BlockSpec((tm, WIDTH_A), lambda i, j: (i, 0)),
                  pl.BlockSpec((tm, WIDTH_B), lambda i, j: (i, 0)),
                  pl.BlockSpec((tm, tn), lambda i, j: (i, j)),
                  pl.BlockSpec((tm, tn), lambda i, j: (i, j + nb)),
                  pl.BlockSpec((WIDTH_A, tn), lambda i, j: (0, j)),
                  pl.BlockSpec((WIDTH_B, tn), lambda i, j: (0, j))],
        out_specs=pl.BlockSpec((tm, tn), lambda i, j: (i, j)),
        out_shape=jax.ShapeDtypeStruct((m, n), BF16),
        compiler_params=_cparams(("parallel", "arbitrary")),
        name="merge_branches",
    )(att, rwk, gates, gates, wa, wb)


def _mm_norm_res_kernel(a_ref, w_ref, x_ref, g_ref, o_ref, acc_ref):
    kk = pl.program_id(1)

    @pl.when(kk == 0)
    def _():
        acc_ref[...] = jnp.zeros(acc_ref.shape, F32)

    acc_ref[...] += _dot(a_ref[...], w_ref[...])

    @pl.when(kk == pl.num_programs(1) - 1)
    def _():
        o_ref[...] = x_ref[...] + _rms(acc_ref[...], g_ref[...], NORM_EPS)


def matmul_norm_residual(a, w, x, g, tm, tk, name):
    m, kdim = a.shape
    n = w.shape[1]
    return pl.pallas_call(
        _mm_norm_res_kernel,
        grid=(m // tm, kdim // tk),
        in_specs=[pl.BlockSpec((tm, tk), lambda i, k: (i, k)),
                  pl.BlockSpec((tk, n), lambda i, k: (k, 0)),
                  pl.BlockSpec((tm, n), lambda i, k: (i, 0)),
                  pl.BlockSpec((1, n), lambda i, k: (0, 0))],
        out_specs=pl.BlockSpec((tm, n), lambda i, k: (i, 0)),
        out_shape=jax.ShapeDtypeStruct((m, n), F32),
        scratch_shapes=[pltpu.VMEM((tm, n), F32)],
        compiler_params=_cparams(("parallel", "arbitrary")),
        name=name,
    )(a, w, x, g.reshape(1, n))


def _gate_up_kernel(h_ref, wg_ref, wu_ref, o_ref):
    h = h_ref[...]
    gt = _dot(h, wg_ref[...])
    up = _dot(h, wu_ref[...])
    o_ref[...] = (gt * _sigmoid(gt) * up).astype(o_ref.dtype)


def ffn_gate_up(h, wg, wu, tm, tn):
    m, kdim = h.shape
    n = wg.shape[1]
    return pl.pallas_call(
        _gate_up_kernel,
        grid=(m // tm, n // tn),
        in_specs=[pl.BlockSpec((tm, kdim), lambda i, j: (i, 0)),
                  pl.BlockSpec((kdim, tn), lambda i, j: (0, j)),
                  pl.BlockSpec((kdim, tn), lambda i, j: (0, j))],
        out_specs=pl.BlockSpec((tm, tn), lambda i, j: (i, j)),
        out_shape=jax.ShapeDtypeStruct((m, n), BF16),
        compiler_params=_cparams(("parallel", "arbitrary")),
        name="ffn_gate_up",
    )(h, wg, wu)


def _pad_lora_cols(x):
    o = 3 * WIDTH_B
    z = jnp.zeros(x.shape[:-1] + (LORA_PAD - R_DECAY,), x.dtype)
    return jnp.concatenate([x[..., :o], x[..., o:o + R_DECAY], z,
                            x[..., o + R_DECAY:o + R_DECAY + R_AAA], z,
                            x[..., o + R_DECAY + R_AAA:]], axis=-1)


def _unpad_lora_cols(x):
    o = 3 * WIDTH_B
    return jnp.concatenate([x[..., :o], x[..., o:o + R_DECAY], x[..., o + LORA_PAD:o + LORA_PAD + R_AAA],
                            x[..., o + 2 * LORA_PAD:]], axis=-1)


def _pad_rows(w, rows):
    return jnp.concatenate([w, jnp.zeros((rows - w.shape[0],) + w.shape[1:], w.dtype)], axis=0)


def _group(x, pos, prev_rows, s0, n_seq, wts, tiles, q_dtype, attention_fn):
    tm = tiles["tm"]
    h = rmsnorm_bf16(x, wts["norm_mix_pre"], tm)
    tables = rope_tables(pos)
    (q,) = projection(h, wts["w_q"], [q_dtype], tm, tiles["tn"], tables, D_HEAD_A ** -0.5, name="proj_q")
    k, k16 = projection(h, wts["w_k"], [F32, BF16], tm, tiles["tn"], tables, name="proj_k")
    v, v16 = projection(h, wts["w_v"], [F32, BF16], tm, tiles["tn"], name="proj_v")
    (rw,) = projection(h, wts["w_rw"], [F32], tm, tiles["tn"], name="proj_rw")
    (gates,) = projection(h, wts["w_gates"], [F32], tm, tiles["tn"], name="proj_gates")

    att = attention_fn(q, k, v, k16, v16)

    prev_seq = prev_rows(rw)
    prep = rwkv_prep(rw, prev_seq, wts["mu"], wts["w0"], wts["a0"], wts["k_k"], wts["k_a"],
                     wts["w_decay"], wts["w_aaa"], wts["w_gate"], tiles["tm_prep"])
    m = x.shape[0]
    t_seq = m // n_seq
    if t_seq % CHUNK:
        pad = CHUNK - t_seq % CHUNK
        prep = [jnp.pad(p.reshape(n_seq, t_seq, WIDTH_B), ((0, 0), (0, pad), (0, 0))).reshape(-1, WIDTH_B)
                for p in prep]
    rwk, s_bd = rwkv_chunked(*prep, wts["lnx_w"], wts["lnx_b"], wts["r_k"], _pair_states_to_bd(s0), n_seq,
                             tiles["rwkv_pairs"])
    if t_seq % CHUNK:
        rwk = rwk.reshape(n_seq, -1, WIDTH_B)[:, :t_seq].reshape(m, WIDTH_B)
    s_new = _bd_to_pair_states(s_bd)

    mrg = merge_branches(att, rwk, gates, wts["w_branch_a"], wts["w_branch_b"], tm, tiles["tn"])
    x1 = matmul_norm_residual(mrg, wts["w_out"], x, wts["norm_mix_post"], tiles["tm_out"], tiles["tk_out"], "out_proj")
    hf = rmsnorm_bf16(x1, wts["norm_ffn_pre"], tm)
    act = ffn_gate_up(hf, wts["w_ffn_gate"], wts["w_ffn_up"], tm, tiles["tn_ffn"])
    y = matmul_norm_residual(act, wts["w_ffn_down"], x1, wts["norm_ffn_post"], tiles["tm_out"], tiles["tk_ffn"], "ffn_down")
    return y, k, v, rw, s_new


def kernel(x_prompt, x_sample, cache_k, cache_v, state_rwkv, state_shift, page_table, norm_mix_pre, w_in, lambda_q1, lambda_k1, lambda_q2, lambda_k2, subln_w, rw_mu, w0, w_decay, a0, w_aaa, w_gate_lora, k_k, k_a, r_k, lnx_w, lnx_b, w_branch_a, w_branch_b, w_out, norm_mix_post, norm_ffn_pre, w_ffn_gate, w_ffn_up, w_ffn_down, norm_ffn_post):
    depth = w_in.shape[0]
    bsz, seq = x_prompt.shape[0], x_prompt.shape[1]
    db, ds = x_sample.shape[0], x_sample.shape[1]
    assert depth == 1 and bsz == 1 and ds == 1
    n_pool = cache_k.shape[1]
    n_pages = page_table.shape[1]
    past = n_pages * PAGE_SIZE
    l = 0
    lambda_init = 0.8 - 0.6 * math.exp(-0.3 * l)

    wi = w_in[l]
    row = lambda p: p.reshape(1, -1)
    wts = {
        "norm_mix_pre": norm_mix_pre[l],
        "w_q": wi[:, :WIDTH_A].astype(BF16),
        "w_k": wi[:, WIDTH_A:2 * WIDTH_A].astype(BF16),
        "w_v": wi[:, 2 * WIDTH_A:A_COLS].astype(BF16),
        "w_rw": _pad_lora_cols(wi[:, A_COLS:A_COLS + RW_COLS]).astype(BF16),
        "w_gates": wi[:, A_COLS + RW_COLS:].astype(BF16),
        "mu": _pad_lora_cols(row(rw_mu[l])),
        "w0": row(w0[l]), "a0": row(a0[l]), "k_k": row(k_k[l]), "k_a": row(k_a[l]),
        "w_decay": _pad_rows(w_decay[l], LORA_PAD).astype(BF16),
        "w_aaa": _pad_rows(w_aaa[l], LORA_PAD).astype(BF16),
        "w_gate": w_gate_lora[l].astype(BF16),
        "lnx_w": lnx_w[l], "lnx_b": lnx_b[l], "r_k": r_k[l],
        "w_branch_a": w_branch_a[l].astype(BF16), "w_branch_b": w_branch_b[l].astype(BF16),
        "w_out": w_out[l].astype(BF16), "norm_mix_post": norm_mix_post[l],
        "norm_ffn_pre": norm_ffn_pre[l],
        "w_ffn_gate": w_ffn_gate[l].astype(BF16), "w_ffn_up": w_ffn_up[l].astype(BF16),
        "w_ffn_down": w_ffn_down[l].astype(BF16), "norm_ffn_post": norm_ffn_post[l],
    }
    lam_params = [row(p[l]) for p in (lambda_q1, lambda_k1, lambda_q2, lambda_k2)]
    subln = subln_w[l]

    tiles_p = dict(tm=512, tn=512, tm_prep=256, tm_out=512, tk_out=512, tn_ffn=512, tk_ffn=512, rwkv_pairs=8)

    def prompt_attention(q, k, v, k16, v16):
        del k, v
        return prompt_diff_attention(q, k16, v16, lam_params, subln, lambda_init, 512, 512)

    def prompt_prev(rw):
        return jnp.concatenate([jnp.zeros((1, RW_PAD_COLS), rw.dtype), rw[:-1]], axis=0)

    s0_p = jnp.zeros((bsz, N_HEADS_B, D_HEAD_B, D_HEAD_B), F32)
    y_p, k_p, v_p, rw_p, s_p = _group(x_prompt.reshape(seq, D_MODEL), jnp.arange(seq), prompt_prev, s0_p, 1,
                                      wts, tiles_p, BF16, prompt_attention)

    tiles_s = dict(tm=db, tn=512, tm_prep=db, tm_out=db, tk_out=512, tn_ffn=512, tk_ffn=512, rwkv_pairs=8)
    ck = jnp.transpose(cache_k[l], (0, 2, 3, 1)).reshape(n_pool, WIDTH_A, PAGE_SIZE)
    cv = cache_v[l]

    def sample_attention(q, k, v, k16, v16):
        del k16, v16
        return sample_diff_attention(q, k, v, ck, cv, page_table, lam_params, subln, lambda_init, 8)

    def sample_prev(rw):
        del rw
        return _pad_lora_cols(state_shift[l])

    y_s, k_s, v_s, rw_s, s_s = _group(x_sample.reshape(db, D_MODEL), jnp.full((db,), past, jnp.int32), sample_prev,
                                      state_rwkv[l], db, wts, tiles_s, F32, sample_attention)

    return (y_p.reshape(bsz, seq, D_MODEL),
            y_s.reshape(db, ds, D_MODEL),
            k_p.reshape(1, bsz, seq, 2 * N_HEADS_A, D_HEAD_A),
            v_p.reshape(1, bsz, seq, N_HEADS_A, 2 * D_HEAD_A),
            s_p.reshape(1, bsz, N_HEADS_B, D_HEAD_B, D_HEAD_B),
            _unpad_lora_cols(rw_p[-1:]).reshape(1, bsz, RW_COLS),
            k_s.reshape(1, db, ds, 2 * N_HEADS_A, D_HEAD_A),
            v_s.reshape(1, db, ds, N_HEADS_A, 2 * D_HEAD_A),
            s_s.reshape(1, db, N_HEADS_B, D_HEAD_B, D_HEAD_B),
            _unpad_lora_cols(rw_s).reshape(1, db, RW_COLS))
```

```python
import functools
import math

import jax
import jax.numpy as jnp
from jax import lax
from jax.experimental import pallas as pl
from jax.experimental.pallas import tpu as pltpu

F32 = jnp.float32
BF16 = jnp.bfloat16

D_MODEL = 2048
N_HEADS_A = 8
D_HEAD_A = 64
WIDTH_A = 2 * N_HEADS_A * D_HEAD_A
N_HEADS_B = 16
D_HEAD_B = 64
WIDTH_B = N_HEADS_B * D_HEAD_B
R_DECAY = 96
R_AAA = 96
R_GATE = 256
ROPE_THETA = 10000.0
NORM_EPS = 1e-6
SUBLN_EPS = 1e-5
GN_EPS = 64e-5
LOG2_E = math.log2(math.e)
PAGE_SIZE = 128
A_COLS = 3 * WIDTH_A
RW_COLS = 3 * WIDTH_B + R_DECAY + R_AAA + R_GATE

LANES = 128
SUBLANES = 8
ONES_ROWS = 16
LORA_PAD = 128
RW_PAD_COLS = 3 * WIDTH_B + 2 * LORA_PAD + R_GATE
CHUNK = 64
INV_BLOCK = 16
VMEM_LIMIT = 56 * 1024 * 1024


def _cparams(sem):
    return pltpu.CompilerParams(dimension_semantics=sem, vmem_limit_bytes=VMEM_LIMIT)


def _dot(a, b):
    return jnp.dot(a, b, preferred_element_type=F32)


def _dot_nt(a, b):
    return lax.dot_general(a, b, (((1,), (1,)), ((), ())), preferred_element_type=F32)


def _dot_tn(a, b):
    return lax.dot_general(a, b, (((0,), (0,)), ((), ())), preferred_element_type=F32)


def _sigmoid(x):
    return 1.0 / (1.0 + jnp.exp(-x))


def _rms(x, g, eps):
    return x * lax.rsqrt(jnp.mean(x * x, axis=-1, keepdims=True) + eps) * g


def _norm_kernel(x_ref, g_ref, o_ref):
    o_ref[...] = _rms(x_ref[...], g_ref[...], NORM_EPS).astype(o_ref.dtype)


def rmsnorm_bf16(x, g, tm):
    m, d = x.shape
    return pl.pallas_call(
        _norm_kernel,
        grid=(m // tm,),
        in_specs=[pl.BlockSpec((tm, d), lambda i: (i, 0)),
                  pl.BlockSpec((1, d), lambda i: (0, 0))],
        out_specs=pl.BlockSpec((tm, d), lambda i: (i, 0)),
        out_shape=jax.ShapeDtypeStruct((m, d), BF16),
        compiler_params=_cparams(("parallel",)),
        name="rmsnorm_bf16",
    )(x, g.reshape(1, d))


def _rope_tile(x, cos, sin_signed):
    lane = lax.broadcasted_iota(jnp.int32, x.shape, 1)
    first_half = (lane % D_HEAD_A) < (D_HEAD_A // 2)
    partner = jnp.where(first_half,
                        pltpu.roll(x, LANES - D_HEAD_A // 2, 1),
                        pltpu.roll(x, D_HEAD_A // 2, 1))
    return x * cos + partner * sin_signed


def _proj_kernel(*refs, rope, scale, n_out):
    if rope:
        a_ref, w_ref, cos_ref, sin_ref = refs[:4]
        outs = refs[4:]
    else:
        a_ref, w_ref = refs[:2]
        outs = refs[2:]
    acc = _dot(a_ref[...], w_ref[...])
    if rope:
        cos = cos_ref[...]
        sin = sin_ref[...]
        tn = acc.shape[1]
        acc = jnp.concatenate(
            [_rope_tile(acc[:, c * LANES:(c + 1) * LANES], cos, sin) for c in range(tn // LANES)], axis=1)
    if scale != 1.0:
        acc = acc * scale
    for o_ref in outs[:n_out]:
        o_ref[...] = acc.astype(o_ref.dtype)


def projection(a, w, out_dtypes, tm, tn, rope_tables=None, scale=1.0, name="projection"):
    m, kdim = a.shape
    n = w.shape[1]
    in_specs = [pl.BlockSpec((tm, kdim), lambda i, j: (i, 0)),
                pl.BlockSpec((kdim, tn), lambda i, j: (0, j))]
    args = [a, w]
    if rope_tables is not None:
        in_specs += [pl.BlockSpec((tm, LANES), lambda i, j: (i, 0))] * 2
        args += list(rope_tables)
    outs = pl.pallas_call(
        functools.partial(_proj_kernel, rope=rope_tables is not None, scale=scale, n_out=len(out_dtypes)),
        grid=(m // tm, n // tn),
        in_specs=in_specs,
        out_specs=[pl.BlockSpec((tm, tn), lambda i, j: (i, j)) for _ in out_dtypes],
        out_shape=[jax.ShapeDtypeStruct((m, n), dt) for dt in out_dtypes],
        compiler_params=_cparams(("parallel", "arbitrary")),
        name=name,
    )(*args)
    return outs


def rope_tables(pos):
    half = D_HEAD_A // 2
    inv = 1.0 / (ROPE_THETA ** (jnp.arange(half, dtype=F32) / half))
    ang = pos.astype(F32)[:, None] * inv[None, :]
    cos = jnp.cos(ang)
    sin = jnp.sin(ang)
    cos2 = jnp.concatenate([cos, cos, cos, cos], axis=1)
    sin2 = jnp.concatenate([-sin, sin, -sin, sin], axis=1)
    return cos2, sin2


def _diff_lambda(lq1, lk1, lq2, lk2, lambda_init):
    s1 = jnp.sum(lq1 * lk1, axis=-1, keepdims=True)
    s2 = jnp.sum(lq2 * lk2, axis=-1, keepdims=True)
    return jnp.exp(s1) - jnp.exp(s2) + lambda_init


def _flash_kernel(q_ref, k_ref, v_ref, lq1_ref, lk1_ref, lq2_ref, lk2_ref, g_ref, o_ref,
                  qs_ref, m_ref, acc_ref, s_ref, p_ref, *, tq, tk, col_chunk, row_chunk, lambda_init):
    qi = pl.program_id(1)
    ratio = tq // tk
    qt = jnp.transpose(q_ref[...].astype(F32))
    qt2 = jnp.concatenate([qt, qt], axis=1)
    row = lax.broadcasted_iota(jnp.int32, qt2.shape, 0)
    col = lax.broadcasted_iota(jnp.int32, qt2.shape, 1)
    qs_ref[...] = jnp.where((row < D_HEAD_A) == (col < tq), qt2, 0.0).astype(BF16)
    m_ref[...] = jnp.full(m_ref.shape, -jnp.inf, F32)
    acc_ref[...] = jnp.zeros(acc_ref.shape, F32)

    def column_chunk(c, k_blk, v_t, kv, masked):
        cols = slice(c * col_chunk, (c + 1) * col_chunk)
        s_ref[:, cols] = _dot(k_blk, qs_ref[:, cols])
        yield
        sub = SUBLANES
        groups = row_chunk // sub

        def scores(r0):
            blk = s_ref[r0:r0 + row_chunk, cols]
            if masked:
                krow = lax.broadcasted_iota(jnp.int32, blk.shape, 0) + r0
                qcol = lax.broadcasted_iota(jnp.int32, blk.shape, 1) + (c * col_chunk) % tq
                blk = jnp.where(kv * tk + krow <= qi * tq + qcol, blk, -jnp.inf)
            return blk

        part = None
        for r0 in range(0, tk, row_chunk):
            mx = jnp.max(scores(r0).reshape(groups, sub, col_chunk), axis=0)
            part = mx if part is None else jnp.maximum(part, mx)
        m_prev = m_ref[:, cols]
        m_new = jnp.maximum(m_prev, jnp.max(part, axis=0, keepdims=True))
        alpha = jnp.exp2(m_prev - m_new)
        for r0 in range(0, tk, row_chunk):
            p_ref[r0:r0 + row_chunk, cols] = jnp.exp2(scores(r0) - m_new).astype(BF16)
        m_ref[:, cols] = m_new
        pv = _dot(v_t, p_ref[:, cols])
        yield
        acc_ref[:, cols] = alpha * acc_ref[:, cols] + pv

    def block(kv, masked):
        start = pl.multiple_of(kv * tk, tk)
        k_blk = k_ref[pl.ds(start, tk), :]
        v_t = jnp.transpose(v_ref[pl.ds(start, tk), :])
        v_t = jnp.concatenate([v_t, jnp.ones((ONES_ROWS, tk), BF16)], axis=0)
        _run_interleaved([column_chunk(c, k_blk, v_t, kv, masked) for c in range(2 * tq // col_chunk)])

    def full_block(kv, carry):
        block(kv, False)
        return carry

    lax.fori_loop(0, qi * ratio, full_block, 0)
    for r in range(ratio):
        block(qi * ratio + r, True)

    lam = _diff_lambda(lq1_ref[...], lk1_ref[...], lq2_ref[...], lk2_ref[...], lambda_init)
    o = acc_ref[0:LANES, :] / acc_ref[LANES:LANES + 1, :]
    o = jnp.transpose(o[:, 0:tq] - lam * o[:, tq:2 * tq])
    o = _rms(o, g_ref[...], SUBLN_EPS) * (1.0 - lambda_init)
    o_ref[...] = o.astype(o_ref.dtype)


def prompt_diff_attention(q, k, v, lam_params, subln_w, lambda_init, tq, tk):
    t = q.shape[0]
    n_pairs = N_HEADS_A
    small = pl.BlockSpec((1, D_HEAD_A), lambda h, i: (0, 0))
    whole = pl.BlockSpec((t, LANES), lambda h, i: (0, h))
    return pl.pallas_call(
        functools.partial(_flash_kernel, tq=tq, tk=tk, col_chunk=2 * LANES, row_chunk=8 * SUBLANES,
                          lambda_init=lambda_init),
        grid=(n_pairs, t // tq),
        in_specs=[pl.BlockSpec((tq, LANES), lambda h, i: (i, h)),
                  whole, whole,
                  small, small, small, small,
                  pl.BlockSpec((1, LANES), lambda h, i: (0, 0))],
        out_specs=pl.BlockSpec((tq, LANES), lambda h, i: (i, h)),
        out_shape=jax.ShapeDtypeStruct((t, WIDTH_A), BF16),
        scratch_shapes=[pltpu.VMEM((LANES, 2 * tq), BF16),
                        pltpu.VMEM((1, 2 * tq), F32),
                        pltpu.VMEM((LANES + ONES_ROWS, 2 * tq), F32),
                        pltpu.VMEM((tk, 2 * tq), F32),
                        pltpu.VMEM((tk, 2 * tq), BF16)],
        compiler_params=_cparams(("parallel", "arbitrary")),
        name="prompt_diff_attention",
    )(q, k, v, *lam_params, subln_w.reshape(1, LANES))


def _decode_kernel(pt_ref, q_ref, kn_ref, vn_ref, lq1_ref, lk1_ref, lq2_ref, lk2_ref, g_ref, *rest,
                   pages_per_step, lambda_init):
    del pt_ref
    k_refs = rest[:pages_per_step]
    v_refs = rest[pages_per_step:2 * pages_per_step]
    o_ref = rest[2 * pages_per_step]
    qm_ref, ex_ref, m_ref, l_ref, acc_ref = rest[2 * pages_per_step + 1:]
    hp = N_HEADS_A
    n_rows = 2 * hp
    j = pl.program_id(1)

    def pair_mask():
        row = lax.broadcasted_iota(jnp.int32, (n_rows, PAGE_SIZE * hp), 0)
        lane = lax.broadcasted_iota(jnp.int32, (n_rows, PAGE_SIZE * hp), 1)
        return (lane % hp) == (row % hp)

    @pl.when(j == 0)
    def _():
        row = lax.broadcasted_iota(jnp.int32, (n_rows, WIDTH_A), 0)
        lane = lax.broadcasted_iota(jnp.int32, (n_rows, WIDTH_A), 1)
        qb = jnp.broadcast_to(q_ref[0], (n_rows, WIDTH_A))
        qm_ref[...] = jnp.where(lane // D_HEAD_A == 2 * (row % hp) + row // hp, qb, 0.0)
        tok = lax.broadcasted_iota(jnp.int32, ex_ref.shape, 0)
        pos = lax.broadcasted_iota(jnp.int32, ex_ref.shape, 1)
        ex_ref[...] = jnp.where(pos // hp == tok, 1.0, 0.0).astype(BF16)
        m_ref[...] = jnp.full(m_ref.shape, -jnp.inf, F32)
        l_ref[...] = jnp.zeros(l_ref.shape, F32)
        acc_ref[...] = jnp.zeros(acc_ref.shape, F32)

    qm = qm_ref[...].astype(BF16)
    keep = pair_mask()
    pages = range(pages_per_step)
    s = jnp.concatenate([_dot(qm, k_refs[i][0].astype(BF16)) for i in pages], axis=1)
    m_prev = m_ref[...]
    m_new = jnp.maximum(m_prev, jnp.max(s, axis=-1, keepdims=True))
    alpha = jnp.exp(m_prev - m_new)
    p = jnp.exp(s - m_new)
    l_ref[...] = alpha * l_ref[...] + jnp.sum(p, axis=-1, keepdims=True)
    p16 = p.astype(BF16)
    p_wide = [jnp.where(keep, _dot(p16[:, i * PAGE_SIZE:(i + 1) * PAGE_SIZE], ex_ref[...]), 0.0).astype(BF16)
              for i in pages]
    pv = _dot(p_wide[0], v_refs[0][0].reshape(PAGE_SIZE * hp, LANES).astype(BF16))
    for i in pages[1:]:
        pv += _dot(p_wide[i], v_refs[i][0].reshape(PAGE_SIZE * hp, LANES).astype(BF16))
    acc_ref[...] = alpha * acc_ref[...] + pv
    m_ref[...] = m_new

    @pl.when(j == pl.num_programs(1) - 1)
    def _():
        s = jnp.sum(qm_ref[...] * kn_ref[0], axis=-1, keepdims=True)
        m_prev = m_ref[...]
        m_new = jnp.maximum(m_prev, s)
        alpha = jnp.exp(m_prev - m_new)
        p = jnp.exp(s - m_new)
        l = alpha * l_ref[...] + p
        vn = vn_ref[0]
        acc = alpha * acc_ref[...] + p * jnp.concatenate([vn, vn], axis=0)
        lam = _diff_lambda(lq1_ref[...], lk1_ref[...], lq2_ref[...], lk2_ref[...], lambda_init)
        o = acc / l
        o = o[0:hp, :] - lam * o[hp:n_rows, :]
        o_ref[0] = (_rms(o, g_ref[...], SUBLN_EPS) * (1.0 - lambda_init)).astype(o_ref.dtype)


def sample_diff_attention(q, k_new, v_new, cache_k, cache_v, page_table, lam_params, subln_w,
                          lambda_init, pages_per_step):
    b = q.shape[0]
    n_pages = page_table.shape[1]
    steps = n_pages // pages_per_step
    hp = N_HEADS_A
    row3 = lambda x: x.reshape(b, 1, WIDTH_A)
    vec = pl.BlockSpec((1, 1, WIDTH_A), lambda bi, j, pt: (bi, 0, 0))
    small = pl.BlockSpec((1, D_HEAD_A), lambda bi, j, pt: (0, 0))
    pairs = pl.BlockSpec((1, hp, LANES), lambda bi, j, pt: (bi, 0, 0))

    def page_spec(i, shape):
        zeros = (0,) * len(shape)
        return pl.BlockSpec((1,) + shape,
                            lambda bi, j, pt: (pt[bi * n_pages + j * pages_per_step + i],) + zeros)

    grid_spec = pltpu.PrefetchScalarGridSpec(
        num_scalar_prefetch=1,
        grid=(b, steps),
        in_specs=[vec, vec, pairs, small, small, small, small,
                  pl.BlockSpec((1, LANES), lambda bi, j, pt: (0, 0))]
                 + [page_spec(i, (WIDTH_A, PAGE_SIZE)) for i in range(pages_per_step)]
                 + [page_spec(i, (PAGE_SIZE, hp, LANES)) for i in range(pages_per_step)],
        out_specs=pairs,
        scratch_shapes=[pltpu.VMEM((2 * hp, WIDTH_A), F32),
                        pltpu.VMEM((PAGE_SIZE, PAGE_SIZE * hp), BF16),
                        pltpu.VMEM((2 * hp, 1), F32),
                        pltpu.VMEM((2 * hp, 1), F32),
                        pltpu.VMEM((2 * hp, LANES), F32)],
    )
    out = pl.pallas_call(
        functools.partial(_decode_kernel, pages_per_step=pages_per_step, lambda_init=lambda_init),
        grid_spec=grid_spec,
        out_shape=jax.ShapeDtypeStruct((b, hp, LANES), BF16),
        compiler_params=_cparams(("parallel", "arbitrary")),
        name="sample_diff_attention",
    )(page_table.reshape(-1), row3(q), row3(k_new), v_new.reshape(b, hp, LANES), *lam_params,
      subln_w.reshape(1, LANES), *([cache_k] * pages_per_step), *([cache_v] * pages_per_step))
    return out.reshape(b, WIDTH_A)


def _rwkv_prep_kernel(c_ref, p_ref, mu_ref, w0_ref, a0_ref, kk_ref, ka_ref, wdec_ref, waaa_ref, wgate_ref,
                      r_ref, lw_ref, k_ref, v_ref, kkr_ref, a_ref, g_ref, *, one_sequence):
    c = c_ref[...]
    if one_sequence:
        above = jnp.where(pl.program_id(0) == 0, 0.0, p_ref[SUBLANES - 1:SUBLANES, :])
        row = lax.broadcasted_iota(jnp.int32, c.shape, 0)
        prev = jnp.where(row == 0, above, pltpu.roll(c, 1, 0))
    else:
        prev = p_ref[...]
    u = c + mu_ref[...] * (prev - c)
    wb = WIDTH_B
    r = u[:, 0:wb]
    kx = u[:, wb:2 * wb]
    v = u[:, 2 * wb:3 * wb]
    o = 3 * wb
    wd = u[:, o:o + LORA_PAD]
    ad = u[:, o + LORA_PAD:o + 2 * LORA_PAD]
    gd = u[:, o + 2 * LORA_PAD:]
    z = w0_ref[...] + _dot(jnp.tanh(wd).astype(BF16), wdec_ref[...])
    nz = -z
    softplus = jnp.maximum(nz, 0.0) + jnp.log(1.0 + jnp.exp(-jnp.abs(nz)))
    w_raw = -softplus - 0.5
    a = _sigmoid(a0_ref[...] + _dot(ad.astype(BF16), waaa_ref[...]))
    r_ref[...] = r
    lw_ref[...] = -jnp.exp(w_raw)
    k_ref[...] = kx * (1.0 + (a - 1.0) * ka_ref[...])
    v_ref[...] = v
    kkr_ref[...] = kx * kk_ref[...]
    a_ref[...] = a
    g_ref[...] = _dot(_sigmoid(gd).astype(BF16), wgate_ref[...])


def rwkv_prep(cols, prev_rows, mu, w0, a0, k_k, k_a, w_decay, w_aaa, w_gate, tm):
    m = cols.shape[0]
    wb = WIDTH_B
    row = lambda n: pl.BlockSpec((1, n), lambda i: (0, 0))
    full = lambda r_, c_: pl.BlockSpec((r_, c_), lambda i: (0, 0))
    tile = lambda n: pl.BlockSpec((tm, n), lambda i: (i, 0))
    one_sequence = prev_rows is None
    if one_sequence:
        prev_rows = cols
        prev_spec = pl.BlockSpec((SUBLANES, RW_PAD_COLS),
                                 lambda i: (jnp.maximum(i * (tm // SUBLANES) - 1, 0), 0))
    else:
        prev_spec = tile(RW_PAD_COLS)
    return pl.pallas_call(
        functools.partial(_rwkv_prep_kernel, one_sequence=one_sequence),
        grid=(m // tm,),
        in_specs=[tile(RW_PAD_COLS), prev_spec, row(RW_PAD_COLS), row(wb), row(wb), row(wb), row(wb),
                  full(LORA_PAD, wb), full(LORA_PAD, wb), full(R_GATE, wb)],
        out_specs=[tile(wb)] * 7,
        out_shape=[jax.ShapeDtypeStruct((m, wb), F32)] * 7,
        compiler_params=_cparams(("parallel",)),
        name="rwkv_prep",
    )(cols, prev_rows, mu, w0, a0, k_k, k_a, w_decay, w_aaa, w_gate)


def _rwkv_pair_chunk(r, lw, k, v, kkr, a, g, lnw, lnb, rk, s_prev):
    cl = CHUNK
    n = D_HEAD_B

    def stack(x):
        x2 = jnp.concatenate([x, x], axis=0)
        row = lax.broadcasted_iota(jnp.int32, x2.shape, 0)
        lane = lax.broadcasted_iota(jnp.int32, x2.shape, 1)
        return jnp.where((row // cl) == (lane // n), x2, 0.0)

    def fold(x_bd):
        return x_bd[0:cl, :] + x_bd[cl:2 * cl, :]

    row_c = lax.broadcasted_iota(jnp.int32, lw.shape, 0)
    cum = lw
    shift = 1
    while shift < cl:
        cum = cum + jnp.where(row_c >= shift, pltpu.roll(cum, shift, 0), 0.0)
        shift *= 2
    cum_last = cum[cl - 1:cl, :]

    kk_bd = stack(kkr)
    norm = jnp.sqrt(jnp.sum(kk_bd * kk_bd, axis=-1, keepdims=True))
    kk = fold(kk_bd * (1.0 / jnp.maximum(norm, 1e-12)))
    b = kk * a

    e_pos = jnp.exp(cum)
    e_neg = jnp.exp(-cum)
    e_rest = jnp.exp(cum_last - cum)
    r_t = stack(r * e_pos)
    a_t = stack(-kk * jnp.exp(cum - lw))
    b_t = b * e_neg
    k_t = k * e_neg
    b_h = stack(b * e_rest)
    k_h = stack(k * e_rest)
    v_bd = stack(v)
    gamma = jnp.exp(cum_last)

    lhs = jnp.concatenate([a_t, r_t], axis=0).astype(BF16)
    rhs = jnp.concatenate([b_t, b_t, k_t, k_t], axis=0).astype(BF16)
    m1 = _dot_nt(lhs, rhs)
    xr = _dot_nt(lhs, s_prev.astype(BF16))
    yield
    c2 = 2 * cl
    row = lax.broadcasted_iota(jnp.int32, (c2, c2), 0)
    col = lax.broadcasted_iota(jnp.int32, (c2, c2), 1)
    same_head = (row // cl) == (col // cl)
    strict = jnp.logical_and(same_head, row > col)
    incl = jnp.logical_and(same_head, row >= col)
    a_ab = jnp.where(strict, m1[0:c2, 0:c2], 0.0)
    a_ak = jnp.where(strict, m1[0:c2, c2:2 * c2], 0.0)
    a_rb = jnp.where(incl, m1[c2:2 * c2, 0:c2], 0.0)
    a_rk = jnp.where(incl, m1[c2:2 * c2, c2:2 * c2], 0.0)

    mm = lambda x, y: _dot(x.astype(BF16), y.astype(BF16))
    eye = jnp.where(row == col, 1.0, 0.0)
    diag_blk = (row // INV_BLOCK) == (col // INV_BLOCK)
    x = jnp.where(diag_blk, a_ab, 0.0)
    a_off = a_ab - x
    t_d = eye + x
    vv = mm(jnp.concatenate([a_ak, a_rk], axis=0), v_bd)
    span = 2
    while span < INV_BLOCK:
        x = mm(x, x)
        yield
        t_d = t_d + mm(t_d, x)
        span *= 2
    yield
    nm = mm(t_d, a_off)
    yield
    nm_pow = nm
    m_acc = eye + nm
    blocks = 2
    while blocks < cl // INV_BLOCK:
        sq = mm(nm_pow, nm_pow)
        yield
        m_acc = m_acc + mm(m_acc, sq)
        yield
        nm_pow = sq
        blocks *= 2
    t_inv = mm(m_acc, t_d)
    yield
    u = mm(t_inv, xr[0:c2, :] + vv[0:c2, :])
    yield
    y_bd = xr[c2:2 * c2, :] + mm(a_rb, u) + vv[c2:2 * c2, :]
    s_new = s_prev * gamma + _dot_tn(jnp.concatenate([u, v_bd], axis=0).astype(BF16),
                                     jnp.concatenate([b_h, k_h], axis=0).astype(BF16))
    yield

    lane2 = lax.broadcasted_iota(jnp.int32, (c2, LANES), 1)
    row2 = lax.broadcasted_iota(jnp.int32, (c2, LANES), 0)
    head_mask = (row2 // cl) == (lane2 // n)
    mean = jnp.sum(y_bd, axis=-1, keepdims=True) * (1.0 / n)
    d = jnp.where(head_mask, y_bd - mean, 0.0)
    var = jnp.sum(d * d, axis=-1, keepdims=True) * (1.0 / n)
    yn = fold(d * lax.rsqrt(var + GN_EPS))
    yn = yn * lnw + lnb
    rk_sum = jnp.sum(stack(r * k * rk), axis=-1, keepdims=True)
    bonus = fold(rk_sum * v_bd)
    return (yn + bonus) * g, s_new


def _run_interleaved(generators):
    results = [None] * len(generators)
    active = list(enumerate(generators))
    while active:
        still = []
        for i, gen in active:
            try:
                next(gen)
                still.append((i, gen))
            except StopIteration as done:
                results[i] = done.value
        active = still
    return results


def _rwkv_chunk_kernel(r_ref, lw_ref, k_ref, v_ref, kkr_ref, a_ref, g_ref, lnw_ref, lnb_ref, rk_ref, s0_ref,
                       y_ref, s_out_ref, s_ref, *, pairs_per_step):
    c = pl.program_id(2)

    @pl.when(c == 0)
    def _():
        s_ref[...] = s0_ref[0]

    in_refs = (r_ref, lw_ref, k_ref, v_ref, kkr_ref, a_ref, g_ref, lnw_ref, lnb_ref, rk_ref)
    lanes = [slice(p * LANES, (p + 1) * LANES) for p in range(pairs_per_step)]
    loaded = [tuple(ref[:, ln] for ref in in_refs) + (s_ref[p],) for p, ln in enumerate(lanes)]
    results = _run_interleaved([_rwkv_pair_chunk(*args) for args in loaded])
    for p, (y, s_new) in enumerate(results):
        y_ref[:, lanes[p]] = y.astype(y_ref.dtype)
        s_ref[p] = s_new

    @pl.when(c == pl.num_programs(2) - 1)
    def _():
        s_out_ref[0] = s_ref[...]


def rwkv_chunked(r, lw, k, v, kkr, a, g, lnx_w, lnx_b, r_k, s0_bd, n_seq, pairs_per_step):
    m = r.shape[0]
    n_pairs = N_HEADS_B // 2
    nc = m // n_seq // CHUNK
    pps = pairs_per_step
    tile = pl.BlockSpec((CHUNK, pps * LANES), lambda s, h, c: (s * nc + c, h))
    par = pl.BlockSpec((1, pps * LANES), lambda s, h, c: (0, h))
    st = pl.BlockSpec((1, pps, LANES, LANES), lambda s, h, c: (s, h, 0, 0))
    return pl.pallas_call(
        functools.partial(_rwkv_chunk_kernel, pairs_per_step=pps),
        grid=(n_seq, n_pairs // pps, nc),
        in_specs=[tile] * 7 + [par] * 3 + [st],
        out_specs=[tile, st],
        out_shape=[jax.ShapeDtypeStruct((m, WIDTH_B), BF16),
                   jax.ShapeDtypeStruct((n_seq, n_pairs, LANES, LANES), F32)],
        scratch_shapes=[pltpu.VMEM((pps, LANES, LANES), F32)],
        compiler_params=_cparams(("parallel", "parallel", "arbitrary")),
        name="rwkv_chunked",
    )(r, lw, k, v, kkr, a, g, lnx_w.reshape(1, -1), lnx_b.reshape(1, -1), r_k.reshape(1, -1), s0_bd)


def _pair_states_to_bd(s):
    bsz = s.shape[0]
    s = s.reshape(bsz, N_HEADS_B // 2, 2, D_HEAD_B, D_HEAD_B)
    z = jnp.zeros_like(s[:, :, 0])
    top = jnp.concatenate([s[:, :, 0], z], axis=-1)
    bot = jnp.concatenate([z, s[:, :, 1]], axis=-1)
    return jnp.concatenate([top, bot], axis=-2)


def _bd_to_pair_states(s_bd):
    bsz = s_bd.shape[0]
    n = D_HEAD_B
    s0 = s_bd[:, :, 0:n, 0:n]
    s1 = s_bd[:, :, n:2 * n, n:2 * n]
    return jnp.stack([s0, s1], axis=2).reshape(bsz, N_HEADS_B, n, n)


def _merge_kernel(att_ref, rwk_ref, ga_ref, gb_ref, wa_ref, wb_ref, o_ref):
    ya = _dot(att_ref[...], wa_ref[...])
    yb = _dot(rwk_ref[...], wb_ref[...])
    o_ref[...] = (_sigmoid(ga_ref[...]) * ya + _sigmoid(gb_ref[...]) * yb).astype(o_ref.dtype)


def merge_branches(att, rwk, gates, wa, wb, tm, tn):
    m = att.shape[0]
    n = wa.shape[1]
    nb = n // tn
    return pl.pallas_call(
        _merge_kernel,
        grid=(m // tm, nb),
        in_specs=[pl.BlockSpec((tm, WIDTH_A), lambda i, j: (i, 0)),
                  pl.BlockSpec((tm, WIDTH_B), lambda i, j: (i, 0)),
                  pl.BlockSpec((tm, tn), lambda i, j: (i, j)),
                  pl.BlockSpec((tm, tn), lambda i, j: (i, j + nb)),
                  pl.BlockSpec((WIDTH_A, tn), lambda i, j: (0, j)),
                  pl.BlockSpec((WIDTH_B, tn), lambda i, j: (0, j))],
        out_specs=pl.BlockSpec((tm, tn), lambda i, j: (i, j)),
        out_shape=jax.ShapeDtypeStruct((m, n), BF16),
        compiler_params=_cparams(("parallel", "arbitrary")),
        name="merge_branches",
    )(att, rwk, gates, gates, wa, wb)


def _mm_norm_res_kernel(a_ref, w_ref, x_ref, g_ref, o_ref, acc_ref):
    kk = pl.program_id(1)

    @pl.when(kk == 0)
    def _():
        acc_ref[...] = jnp.zeros(acc_ref.shape, F32)

    acc_ref[...] += _dot(a_ref[...], w_ref[...])

    @pl.when(kk == pl.num_programs(1) - 1)
    def _():
        o_ref[...] = x_ref[...] + _rms(acc_ref[...], g_ref[...], NORM_EPS)


def matmul_norm_residual(a, w, x, g, tm, tk, name):
    m, kdim = a.shape
    n = w.shape[1]
    return pl.pallas_call(
        _mm_norm_res_kernel,
        grid=(m // tm, kdim // tk),
        in_specs=[pl.BlockSpec((tm, tk), lambda i, k: (i, k)),
                  pl.BlockSpec((tk, n), lambda i, k: (k, 0)),
                  pl.BlockSpec((tm, n), lambda i, k: (i, 0)),
                  pl.BlockSpec((1, n), lambda i, k: (0, 0))],
        out_specs=pl.BlockSpec((tm, n), lambda i, k: (i, 0)),
        out_shape=jax.ShapeDtypeStruct((m, n), F32),
        scratch_shapes=[pltpu.VMEM((tm, n), F32)],
        compiler_params=_cparams(("parallel", "arbitrary")),
        name=name,
    )(a, w, x, g.reshape(1, n))


def _gate_up_kernel(h_ref, wg_ref, wu_ref, o_ref):
    h = h_ref[...]
    gt = _dot(h, wg_ref[...])
    up = _dot(h, wu_ref[...])
    o_ref[...] = (gt * _sigmoid(gt) * up).astype(o_ref.dtype)


def ffn_gate_up(h, wg, wu, tm, tn):
    m, kdim = h.shape
    n = wg.shape[1]
    return pl.pallas_call(
        _gate_up_kernel,
        grid=(m // tm, n // tn),
        in_specs=[pl.BlockSpec((tm, kdim), lambda i, j: (i, 0)),
                  pl.BlockSpec((kdim, tn), lambda i, j: (0, j)),
                  pl.BlockSpec((kdim, tn), lambda i, j: (0, j))],
        out_specs=pl.BlockSpec((tm, tn), lambda i, j: (i, j)),
        out_shape=jax.ShapeDtypeStruct((m, n), BF16),
        compiler_params=_cparams(("parallel", "arbitrary")),
        name="ffn_gate_up",
    )(h, wg, wu)


def _pad_lora_cols(x):
    o = 3 * WIDTH_B
    z = jnp.zeros(x.shape[:-1] + (LORA_PAD - R_DECAY,), x.dtype)
    return jnp.concatenate([x[..., :o], x[..., o:o + R_DECAY], z,
                            x[..., o + R_DECAY:o + R_DECAY + R_AAA], z,
                            x[..., o + R_DECAY + R_AAA:]], axis=-1)


def _unpad_lora_cols(x):
    o = 3 * WIDTH_B
    return jnp.concatenate([x[..., :o], x[..., o:o + R_DECAY], x[..., o + LORA_PAD:o + LORA_PAD + R_AAA],
                            x[..., o + 2 * LORA_PAD:]], axis=-1)


def _pad_rows(w, rows):
    return jnp.concatenate([w, jnp.zeros((rows - w.shape[0],) + w.shape[1:], w.dtype)], axis=0)


def _tiles(m):
    return dict(
        tm=min(m, 1024),
        tm_norm=min(m, 512),
        tm_prep=min(m, 256),
        tm_out=min(m, 512),
        tn=1024,
        tn_rw=896,
        tn_ffn=512,
        tk_out=D_MODEL,
        tk_ffn=1408,
        rwkv_pairs=N_HEADS_B // 2,
    )


def _group(x, pos, prev_rows, s0, n_seq, wts, tiles, q_dtype, q_scale, attention_fn):
    tm = tiles["tm"]
    h = rmsnorm_bf16(x, wts["norm_mix_pre"], tiles["tm_norm"])
    tables = rope_tables(pos)
    (q,) = projection(h, wts["w_q"], [q_dtype], tm, tiles["tn"], tables, q_scale, name="proj_q")
    k, k16 = projection(h, wts["w_k"], [F32, BF16], tm, tiles["tn"], tables, name="proj_k")
    v, v16 = projection(h, wts["w_v"], [F32, BF16], tm, tiles["tn"], name="proj_v")
    (rw,) = projection(h, wts["w_rw"], [F32], tm, tiles["tn_rw"], name="proj_rw")
    (gates,) = projection(h, wts["w_gates"], [F32], tm, tiles["tn"], name="proj_gates")

    att = attention_fn(q, k, v, k16, v16)

    prep = rwkv_prep(rw, prev_rows, wts["mu"], wts["w0"], wts["a0"], wts["k_k"], wts["k_a"],
                     wts["w_decay"], wts["w_aaa"], wts["w_gate"], tiles["tm_prep"])
    m = x.shape[0]
    t_seq = m // n_seq
    if t_seq % CHUNK:
        pad = CHUNK - t_seq % CHUNK
        prep = [jnp.pad(p.reshape(n_seq, t_seq, WIDTH_B), ((0, 0), (0, pad), (0, 0))).reshape(-1, WIDTH_B)
                for p in prep]
    rwk, s_bd = rwkv_chunked(*prep, wts["lnx_w"], wts["lnx_b"], wts["r_k"], _pair_states_to_bd(s0), n_seq,
                             tiles["rwkv_pairs"])
    if t_seq % CHUNK:
        rwk = rwk.reshape(n_seq, -1, WIDTH_B)[:, :t_seq].reshape(m, WIDTH_B)
    s_new = _bd_to_pair_states(s_bd)

    mrg = merge_branches(att, rwk, gates, wts["w_branch_a"], wts["w_branch_b"], tm, tiles["tn"])
    x1 = matmul_norm_residual(mrg, wts["w_out"], x, wts["norm_mix_post"], tiles["tm_out"], tiles["tk_out"], "out_proj")
    hf = rmsnorm_bf16(x1, wts["norm_ffn_pre"], tiles["tm_norm"])
    act = ffn_gate_up(hf, wts["w_ffn_gate"], wts["w_ffn_up"], tm, tiles["tn_ffn"])
    y = matmul_norm_residual(act, wts["w_ffn_down"], x1, wts["norm_ffn_post"], tiles["tm_out"], tiles["tk_ffn"], "ffn_down")
    return y, k, v, rw, s_new


def kernel(x_prompt, x_sample, cache_k, cache_v, state_rwkv, state_shift, page_table, norm_mix_pre, w_in, lambda_q1, lambda_k1, lambda_q2, lambda_k2, subln_w, rw_mu, w0, w_decay, a0, w_aaa, w_gate_lora, k_k, k_a, r_k, lnx_w, lnx_b, w_branch_a, w_branch_b, w_out, norm_mix_post, norm_ffn_pre, w_ffn_gate, w_ffn_up, w_ffn_down, norm_ffn_post):
    depth = w_in.shape[0]
    bsz, seq = x_prompt.shape[0], x_prompt.shape[1]
    db, ds = x_sample.shape[0], x_sample.shape[1]
    assert depth == 1 and bsz == 1 and ds == 1
    n_pool = cache_k.shape[1]
    n_pages = page_table.shape[1]
    past = n_pages * PAGE_SIZE
    l = 0
    lambda_init = 0.8 - 0.6 * math.exp(-0.3 * l)

    wi = w_in[l]
    row = lambda p: p.reshape(1, -1)
    wts = {
        "norm_mix_pre": norm_mix_pre[l],
        "w_q": wi[:, :WIDTH_A].astype(BF16),
        "w_k": wi[:, WIDTH_A:2 * WIDTH_A].astype(BF16),
        "w_v": wi[:, 2 * WIDTH_A:A_COLS].astype(BF16),
        "w_rw": _pad_lora_cols(wi[:, A_COLS:A_COLS + RW_COLS]).astype(BF16),
        "w_gates": wi[:, A_COLS + RW_COLS:].astype(BF16),
        "mu": _pad_lora_cols(row(rw_mu[l])),
        "w0": row(w0[l]), "a0": row(a0[l]), "k_k": row(k_k[l]), "k_a": row(k_a[l]),
        "w_decay": _pad_rows(w_decay[l], LORA_PAD).astype(BF16),
        "w_aaa": _pad_rows(w_aaa[l], LORA_PAD).astype(BF16),
        "w_gate": w_gate_lora[l].astype(BF16),
        "lnx_w": lnx_w[l], "lnx_b": lnx_b[l], "r_k": r_k[l],
        "w_branch_a": w_branch_a[l].astype(BF16), "w_branch_b": w_branch_b[l].astype(BF16),
        "w_out": w_out[l].astype(BF16), "norm_mix_post": norm_mix_post[l],
        "norm_ffn_pre": norm_ffn_pre[l],
        "w_ffn_gate": w_ffn_gate[l].astype(BF16), "w_ffn_up": w_ffn_up[l].astype(BF16),
        "w_ffn_down": w_ffn_down[l].astype(BF16), "norm_ffn_post": norm_ffn_post[l],
    }
    lam_params = [row(p[l]) for p in (lambda_q1, lambda_k1, lambda_q2, lambda_k2)]
    subln = subln_w[l]

    tiles_p = _tiles(seq)

    def prompt_attention(q, k, v, k16, v16):
        del k, v
        return prompt_diff_attention(q, k16, v16, lam_params, subln, lambda_init, 512, 512)

    s0_p = jnp.zeros((bsz, N_HEADS_B, D_HEAD_B, D_HEAD_B), F32)
    y_p, k_p, v_p, rw_p, s_p = _group(x_prompt.reshape(seq, D_MODEL), jnp.arange(seq), None, s0_p, 1,
                                      wts, tiles_p, BF16, LOG2_E * D_HEAD_A ** -0.5, prompt_attention)

    tiles_s = _tiles(db)
    ck = jnp.transpose(cache_k[l], (0, 2, 3, 1)).reshape(n_pool, WIDTH_A, PAGE_SIZE)
    cv = cache_v[l]

    def sample_attention(q, k, v, k16, v16):
        del k16, v16
        return sample_diff_attention(q, k, v, ck, cv, page_table, lam_params, subln, lambda_init, 8)

    y_s, k_s, v_s, rw_s, s_s = _group(x_sample.reshape(db, D_MODEL), jnp.full((db,), past, jnp.int32),
                                      _pad_lora_cols(state_shift[l]),
                                      state_rwkv[l], db, wts, tiles_s, F32, D_HEAD_A ** -0.5, sample_attention)

    return (y_p.reshape(bsz, seq, D_MODEL),
            y_s.reshape(db, ds, D_MODEL),
            k_p.reshape(1, bsz, seq, 2 * N_HEADS_A, D_HEAD_A),
            v_p.reshape(1, bsz, seq, N_HEADS_A, 2 * D_HEAD_A),
            s_p.reshape(1, bsz, N_HEADS_B, D_HEAD_B, D_HEAD_B),
            _unpad_lora_cols(rw_p[-1:]).reshape(1, bsz, RW_COLS),
            k_s.reshape(1, db, ds, 2 * N_HEADS_A, D_HEAD_A),
            v_s.reshape(1, db, ds, N_HEADS_A, 2 * D_HEAD_A),
            s_s.reshape(1, db, N_HEADS_B, D_HEAD_B, D_HEAD_B),
            _unpad_lora_cols(rw_s).reshape(1, db, RW_COLS))
```

```python
import functools
import math

import jax
import jax.numpy as jnp
from jax import lax
from jax.experimental import pallas as pl
from jax.experimental.pallas import tpu as pltpu

F32 = jnp.float32
BF16 = jnp.bfloat16

D_MODEL = 2048
N_HEADS_A = 8
D_HEAD_A = 64
WIDTH_A = 2 * N_HEADS_A * D_HEAD_A
N_HEADS_B = 16
D_HEAD_B = 64
WIDTH_B = N_HEADS_B * D_HEAD_B
R_DECAY = 96
R_AAA = 96
R_GATE = 256
ROPE_THETA = 10000.0
NORM_EPS = 1e-6
SUBLN_EPS = 1e-5
GN_EPS = 64e-5
LOG2_E = math.log2(math.e)
PAGE_SIZE = 128
A_COLS = 3 * WIDTH_A
RW_COLS = 3 * WIDTH_B + R_DECAY + R_AAA + R_GATE

LANES = 128
SUBLANES = 8
ONES_ROWS = 16
KV_BLOCKS_IN_FLIGHT = 2
LORA_PAD = 128
RW_PAD_COLS = 3 * WIDTH_B + 2 * LORA_PAD + R_GATE
CHUNK = 64
INV_BLOCK = 16
VMEM_LIMIT = 56 * 1024 * 1024


def _cparams(sem):
    return pltpu.CompilerParams(dimension_semantics=sem, vmem_limit_bytes=VMEM_LIMIT)


def _dot(a, b):
    return jnp.dot(a, b, preferred_element_type=F32)


def _dot_nt(a, b):
    return lax.dot_general(a, b, (((1,), (1,)), ((), ())), preferred_element_type=F32)


def _dot_tn(a, b):
    return lax.dot_general(a, b, (((0,), (0,)), ((), ())), preferred_element_type=F32)


def _sigmoid(x):
    return 1.0 / (1.0 + jnp.exp(-x))


def _rms(x, g, eps):
    return x * lax.rsqrt(jnp.mean(x * x, axis=-1, keepdims=True) + eps) * g


def _norm_kernel(x_ref, g_ref, o_ref):
    o_ref[...] = _rms(x_ref[...], g_ref[...], NORM_EPS).astype(o_ref.dtype)


def rmsnorm_bf16(x, g, tm):
    m, d = x.shape
    return pl.pallas_call(
        _norm_kernel,
        grid=(m // tm,),
        in_specs=[pl.BlockSpec((tm, d), lambda i: (i, 0)),
                  pl.BlockSpec((1, d), lambda i: (0, 0))],
        out_specs=pl.BlockSpec((tm, d), lambda i: (i, 0)),
        out_shape=jax.ShapeDtypeStruct((m, d), BF16),
        compiler_params=_cparams(("parallel",)),
        name="rmsnorm_bf16",
    )(x, g.reshape(1, d))


def _rope_tile(x, cos, sin_signed):
    lane = lax.broadcasted_iota(jnp.int32, x.shape, 1)
    first_half = (lane % D_HEAD_A) < (D_HEAD_A // 2)
    partner = jnp.where(first_half,
                        pltpu.roll(x, LANES - D_HEAD_A // 2, 1),
                        pltpu.roll(x, D_HEAD_A // 2, 1))
    return x * cos + partner * sin_signed


def _proj_kernel(*refs, rope, scale, n_out, values_t):
    if rope:
        a_ref, w_ref, cos_ref, sin_ref = refs[:4]
        outs = refs[4:-1]
    else:
        a_ref, w_ref = refs[:2]
        outs = refs[2:-1]
    w16_ref = refs[-1]

    @pl.when(pl.program_id(1) == 0)
    def _():
        w16_ref[...] = w_ref[...].astype(BF16)

    acc = _dot(a_ref[...], w16_ref[...])
    if rope:
        cos = cos_ref[...]
        sin = sin_ref[...]
        tn = acc.shape[1]
        acc = jnp.concatenate(
            [_rope_tile(acc[:, c * LANES:(c + 1) * LANES], cos, sin) for c in range(tn // LANES)], axis=1)
    if scale != 1.0:
        acc = acc * scale
    for o_ref in outs[:n_out]:
        o_ref[...] = acc.astype(o_ref.dtype)
    if values_t:
        t_ref = outs[n_out]
        acc_t = jnp.transpose(acc)
        for grp in range(acc.shape[1] // LANES):
            t_ref[grp, 0:LANES, :] = acc_t[grp * LANES:(grp + 1) * LANES, :].astype(t_ref.dtype)
            t_ref[grp, LANES:, :] = jnp.ones((ONES_ROWS, acc.shape[0]), t_ref.dtype)


def projection(a, w, col0, n, out_dtypes, tm, tn, rope_tables=None, scale=1.0, values_t=False,
               name="projection"):
    m, kdim = a.shape
    j0 = col0 // tn
    in_specs = [pl.BlockSpec((tm, kdim), lambda j, i: (i, 0)),
                pl.BlockSpec((kdim, tn), lambda j, i: (0, j + j0))]
    args = [a, w]
    if rope_tables is not None:
        in_specs += [pl.BlockSpec((tm, LANES), lambda j, i: (i, 0))] * 2
        args += list(rope_tables)
    out_specs = [pl.BlockSpec((tm, tn), lambda j, i: (i, j)) for _ in out_dtypes]
    out_shape = [jax.ShapeDtypeStruct((m, n), dt) for dt in out_dtypes]
    if values_t:
        rows = LANES + ONES_ROWS
        out_specs.append(pl.BlockSpec((tn // LANES, rows, tm), lambda j, i: (j, 0, i)))
        out_shape.append(jax.ShapeDtypeStruct((n // LANES, rows, m), BF16))
    outs = pl.pallas_call(
        functools.partial(_proj_kernel, rope=rope_tables is not None, scale=scale, n_out=len(out_dtypes),
                          values_t=values_t),
        grid=(n // tn, m // tm),
        in_specs=in_specs,
        out_specs=out_specs,
        out_shape=out_shape,
        scratch_shapes=[pltpu.VMEM((kdim, tn), BF16)],
        compiler_params=_cparams(("parallel", "arbitrary")),
        name=name,
    )(*args)
    return outs


def rope_tables(pos):
    half = D_HEAD_A // 2
    inv = 1.0 / (ROPE_THETA ** (jnp.arange(half, dtype=F32) / half))
    ang = pos.astype(F32)[:, None] * inv[None, :]
    cos = jnp.cos(ang)
    sin = jnp.sin(ang)
    cos2 = jnp.concatenate([cos, cos, cos, cos], axis=1)
    sin2 = jnp.concatenate([-sin, sin, -sin, sin], axis=1)
    return cos2, sin2


def _diff_lambda(lq1, lk1, lq2, lk2, lambda_init):
    s1 = jnp.sum(lq1 * lk1, axis=-1, keepdims=True)
    s2 = jnp.sum(lq2 * lk2, axis=-1, keepdims=True)
    return jnp.exp(s1) - jnp.exp(s2) + lambda_init


def _flash_kernel(q_ref, k_ref, v_ref, lq1_ref, lk1_ref, lq2_ref, lk2_ref, g_ref, o_ref,
                  qs_ref, m_ref, acc_ref, s_ref, p_ref, *, tq, tk, col_chunk, row_chunk, lambda_init):
    qi = pl.program_id(1)
    ratio = tq // tk
    qt = jnp.transpose(q_ref[...].astype(F32))
    qt2 = jnp.concatenate([qt, qt], axis=1)
    row = lax.broadcasted_iota(jnp.int32, qt2.shape, 0)
    col = lax.broadcasted_iota(jnp.int32, qt2.shape, 1)
    qs_ref[...] = jnp.where((row < D_HEAD_A) == (col < tq), qt2, 0.0).astype(BF16)
    m_ref[...] = jnp.full(m_ref.shape, -jnp.inf, F32)
    acc_ref[...] = jnp.zeros(acc_ref.shape, F32)

    def column_chunk(c, slot, k_blk, v_t, kv, masked):
        cols = slice(c * col_chunk, (c + 1) * col_chunk)
        s_ref[slot, :, cols] = _dot(k_blk, qs_ref[:, cols])
        yield
        sub = SUBLANES
        groups = row_chunk // sub

        def scores(r0):
            blk = s_ref[slot, r0:r0 + row_chunk, cols]
            if masked:
                krow = lax.broadcasted_iota(jnp.int32, blk.shape, 0) + r0
                qcol = lax.broadcasted_iota(jnp.int32, blk.shape, 1) + (c * col_chunk) % tq
                blk = jnp.where(kv * tk + krow <= qi * tq + qcol, blk, -jnp.inf)
            return blk

        part = None
        for r0 in range(0, tk, row_chunk):
            mx = jnp.max(scores(r0).reshape(groups, sub, col_chunk), axis=0)
            part = mx if part is None else jnp.maximum(part, mx)
        m_prev = m_ref[:, cols]
        m_new = jnp.maximum(m_prev, jnp.max(part, axis=0, keepdims=True))
        alpha = jnp.exp2(m_prev - m_new)
        for r0 in range(0, tk, row_chunk):
            p_ref[slot, r0:r0 + row_chunk, cols] = jnp.exp2(scores(r0) - m_new).astype(BF16)
        m_ref[:, cols] = m_new
        pv = _dot(v_t, p_ref[slot, :, cols])
        yield
        acc_ref[:, cols] = alpha * acc_ref[:, cols] + pv

    def blocks(kvs, masked):
        chunks = []
        for slot, kv in enumerate(kvs):
            start = pl.multiple_of(kv * tk, tk)
            k_blk = k_ref[pl.ds(start, tk), :]
            v_t = v_ref[0, :, pl.ds(start, tk)]
            chunks += [column_chunk(c, slot, k_blk, v_t, kv, masked) for c in range(2 * tq // col_chunk)]
        _run_interleaved(chunks)

    n_full = qi * ratio
    slots = s_ref.shape[0]

    def full_blocks(j, carry):
        blocks([j * slots + s for s in range(slots)], False)
        return carry

    lax.fori_loop(0, n_full // slots, full_blocks, 0)
    for rem in range(1, slots):
        @pl.when(n_full % slots == rem)
        def _(rem=rem):
            blocks([n_full - rem + s for s in range(rem)], False)
    blocks([n_full + r for r in range(ratio)], True)

    lam = _diff_lambda(lq1_ref[...], lk1_ref[...], lq2_ref[...], lk2_ref[...], lambda_init)
    o = acc_ref[0:LANES, :] / acc_ref[LANES:LANES + 1, :]
    o = jnp.transpose(o[:, 0:tq] - lam * o[:, tq:2 * tq])
    o = _rms(o, g_ref[...], SUBLN_EPS) * (1.0 - lambda_init)
    o_ref[...] = o.astype(o_ref.dtype)


def prompt_diff_attention(q, k, v_t, lam_params, subln_w, lambda_init, tq, tk):
    t = q.shape[0]
    n_pairs = N_HEADS_A
    small = pl.BlockSpec((1, D_HEAD_A), lambda h, i: (0, 0))
    whole = pl.BlockSpec((t, LANES), lambda h, i: (0, h))
    whole_t = pl.BlockSpec((1, LANES + ONES_ROWS, t), lambda h, i: (h, 0, 0))
    return pl.pallas_call(
        functools.partial(_flash_kernel, tq=tq, tk=tk, col_chunk=2 * LANES, row_chunk=2 * SUBLANES,
                          lambda_init=lambda_init),
        grid=(n_pairs, t // tq),
        in_specs=[pl.BlockSpec((tq, LANES), lambda h, i: (i, h)),
                  whole, whole_t,
                  small, small, small, small,
                  pl.BlockSpec((1, LANES), lambda h, i: (0, 0))],
        out_specs=pl.BlockSpec((tq, LANES), lambda h, i: (i, h)),
        out_shape=jax.ShapeDtypeStruct((t, WIDTH_A), BF16),
        scratch_shapes=[pltpu.VMEM((LANES, 2 * tq), BF16),
                        pltpu.VMEM((1, 2 * tq), F32),
                        pltpu.VMEM((LANES + ONES_ROWS, 2 * tq), F32),
                        pltpu.VMEM((KV_BLOCKS_IN_FLIGHT, tk, 2 * tq), F32),
                        pltpu.VMEM((KV_BLOCKS_IN_FLIGHT, tk, 2 * tq), BF16)],
        compiler_params=_cparams(("parallel", "arbitrary")),
        name="prompt_diff_attention",
    )(q, k, v_t, *lam_params, subln_w.reshape(1, LANES))


def _decode_kernel(pt_ref, q_ref, kn_ref, vn_ref, lq1_ref, lk1_ref, lq2_ref, lk2_ref, g_ref, *rest,
                   pages_per_step, lambda_init):
    del pt_ref
    k_refs = rest[:pages_per_step]
    v_refs = rest[pages_per_step:2 * pages_per_step]
    o_ref = rest[2 * pages_per_step]
    qm_ref, ex_ref, m_ref, l_ref, acc_ref = rest[2 * pages_per_step + 1:]
    hp = N_HEADS_A
    n_rows = 2 * hp
    j = pl.program_id(1)

    def pair_mask():
        row = lax.broadcasted_iota(jnp.int32, (n_rows, PAGE_SIZE * hp), 0)
        lane = lax.broadcasted_iota(jnp.int32, (n_rows, PAGE_SIZE * hp), 1)
        return (lane % hp) == (row % hp)

    @pl.when(j == 0)
    def _():
        row = lax.broadcasted_iota(jnp.int32, (n_rows, WIDTH_A), 0)
        lane = lax.broadcasted_iota(jnp.int32, (n_rows, WIDTH_A), 1)
        qb = jnp.broadcast_to(q_ref[0], (n_rows, WIDTH_A))
        qm_ref[...] = jnp.where(lane // D_HEAD_A == 2 * (row % hp) + row // hp, qb, 0.0)
        tok = lax.broadcasted_iota(jnp.int32, ex_ref.shape, 0)
        pos = lax.broadcasted_iota(jnp.int32, ex_ref.shape, 1)
        ex_ref[...] = jnp.where(pos // hp == tok, 1.0, 0.0).astype(BF16)
        m_ref[...] = jnp.full(m_ref.shape, -jnp.inf, F32)
        l_ref[...] = jnp.zeros(l_ref.shape, F32)
        acc_ref[...] = jnp.zeros(acc_ref.shape, F32)

    qm = qm_ref[...].astype(BF16)
    keep = pair_mask()
    pages = range(pages_per_step)
    s = jnp.concatenate([_dot(qm, k_refs[i][0].astype(BF16)) for i in pages], axis=1)
    m_prev = m_ref[...]
    m_new = jnp.maximum(m_prev, jnp.max(s, axis=-1, keepdims=True))
    alpha = jnp.exp(m_prev - m_new)
    p = jnp.exp(s - m_new)
    l_ref[...] = alpha * l_ref[...] + jnp.sum(p, axis=-1, keepdims=True)
    p16 = p.astype(BF16)
    p_wide = [jnp.where(keep, _dot(p16[:, i * PAGE_SIZE:(i + 1) * PAGE_SIZE], ex_ref[...]), 0.0).astype(BF16)
              for i in pages]
    pv = _dot(p_wide[0], v_refs[0][0].reshape(PAGE_SIZE * hp, LANES).astype(BF16))
    for i in pages[1:]:
        pv += _dot(p_wide[i], v_refs[i][0].reshape(PAGE_SIZE * hp, LANES).astype(BF16))
    acc_ref[...] = alpha * acc_ref[...] + pv
    m_ref[...] = m_new

    @pl.when(j == pl.num_programs(1) - 1)
    def _():
        s = jnp.sum(qm_ref[...] * kn_ref[0], axis=-1, keepdims=True)
        m_prev = m_ref[...]
        m_new = jnp.maximum(m_prev, s)
        alpha = jnp.exp(m_prev - m_new)
        p = jnp.exp(s - m_new)
        l = alpha * l_ref[...] + p
        vn = vn_ref[0]
        acc = alpha * acc_ref[...] + p * jnp.concatenate([vn, vn], axis=0)
        lam = _diff_lambda(lq1_ref[...], lk1_ref[...], lq2_ref[...], lk2_ref[...], lambda_init)
        o = acc / l
        o = o[0:hp, :] - lam * o[hp:n_rows, :]
        o_ref[0] = (_rms(o, g_ref[...], SUBLN_EPS) * (1.0 - lambda_init)).astype(o_ref.dtype)


def sample_diff_attention(q, k_new, v_new, cache_k, cache_v, page_table, lam_params, subln_w,
                          lambda_init, pages_per_step):
    b = q.shape[0]
    n_pages = page_table.shape[1]
    steps = n_pages // pages_per_step
    hp = N_HEADS_A
    row3 = lambda x: x.reshape(b, 1, WIDTH_A)
    vec = pl.BlockSpec((1, 1, WIDTH_A), lambda bi, j, pt: (bi, 0, 0))
    small = pl.BlockSpec((1, D_HEAD_A), lambda bi, j, pt: (0, 0))
    pairs = pl.BlockSpec((1, hp, LANES), lambda bi, j, pt: (bi, 0, 0))

    def page_spec(i, shape):
        zeros = (0,) * len(shape)
        return pl.BlockSpec((1,) + shape,
                            lambda bi, j, pt: (pt[bi * n_pages + j * pages_per_step + i],) + zeros)

    grid_spec = pltpu.PrefetchScalarGridSpec(
        num_scalar_prefetch=1,
        grid=(b, steps),
        in_specs=[vec, vec, pairs, small, small, small, small,
                  pl.BlockSpec((1, LANES), lambda bi, j, pt: (0, 0))]
                 + [page_spec(i, (WIDTH_A, PAGE_SIZE)) for i in range(pages_per_step)]
                 + [page_spec(i, (PAGE_SIZE, hp, LANES)) for i in range(pages_per_step)],
        out_specs=pairs,
        scratch_shapes=[pltpu.VMEM((2 * hp, WIDTH_A), F32),
                        pltpu.VMEM((PAGE_SIZE, PAGE_SIZE * hp), BF16),
                        pltpu.VMEM((2 * hp, 1), F32),
                        pltpu.VMEM((2 * hp, 1), F32),
                        pltpu.VMEM((2 * hp, LANES), F32)],
    )
    out = pl.pallas_call(
        functools.partial(_decode_kernel, pages_per_step=pages_per_step, lambda_init=lambda_init),
        grid_spec=grid_spec,
        out_shape=jax.ShapeDtypeStruct((b, hp, LANES), BF16),
        compiler_params=_cparams(("parallel", "arbitrary")),
        name="sample_diff_attention",
    )(page_table.reshape(-1), row3(q), row3(k_new), v_new.reshape(b, hp, LANES), *lam_params,
      subln_w.reshape(1, LANES), *([cache_k] * pages_per_step), *([cache_v] * pages_per_step))
    return out.reshape(b, WIDTH_A)


def _rwkv_prep_kernel(c_ref, p_ref, mu_ref, w0_ref, a0_ref, kk_ref, ka_ref, wdec_ref, waaa_ref, wgate_ref,
                      r_ref, lw_ref, k_ref, v_ref, kkr_ref, a_ref, g_ref, *, one_sequence):
    c = c_ref[...]
    if one_sequence:
        above = jnp.where(pl.program_id(0) == 0, 0.0, p_ref[SUBLANES - 1:SUBLANES, :])
        row = lax.broadcasted_iota(jnp.int32, c.shape, 0)
        prev = jnp.where(row == 0, above, pltpu.roll(c, 1, 0))
    else:
        prev = p_ref[...]
    u = c + mu_ref[...] * (prev - c)
    wb = WIDTH_B
    r = u[:, 0:wb]
    kx = u[:, wb:2 * wb]
    v = u[:, 2 * wb:3 * wb]
    o = 3 * wb
    wd = u[:, o:o + LORA_PAD]
    ad = u[:, o + LORA_PAD:o + 2 * LORA_PAD]
    gd = u[:, o + 2 * LORA_PAD:]
    z = w0_ref[...] + _dot(jnp.tanh(wd).astype(BF16), wdec_ref[...])
    nz = -z
    softplus = jnp.maximum(nz, 0.0) + jnp.log(1.0 + jnp.exp(-jnp.abs(nz)))
    w_raw = -softplus - 0.5
    a = _sigmoid(a0_ref[...] + _dot(ad.astype(BF16), waaa_ref[...]))
    r_ref[...] = r
    lw_ref[...] = -jnp.exp(w_raw)
    k_ref[...] = kx * (1.0 + (a - 1.0) * ka_ref[...])
    v_ref[...] = v
    kkr_ref[...] = kx * kk_ref[...]
    a_ref[...] = a
    g_ref[...] = _dot(_sigmoid(gd).astype(BF16), wgate_ref[...])


def rwkv_prep(cols, prev_rows, mu, w0, a0, k_k, k_a, w_decay, w_aaa, w_gate, tm):
    m = cols.shape[0]
    wb = WIDTH_B
    row = lambda n: pl.BlockSpec((1, n), lambda i: (0, 0))
    full = lambda r_, c_: pl.BlockSpec((r_, c_), lambda i: (0, 0))
    tile = lambda n: pl.BlockSpec((tm, n), lambda i: (i, 0))
    one_sequence = prev_rows is None
    if one_sequence:
        prev_rows = cols
        prev_spec = pl.BlockSpec((SUBLANES, RW_PAD_COLS),
                                 lambda i: (jnp.maximum(i * (tm // SUBLANES) - 1, 0), 0))
    else:
        prev_spec = tile(RW_PAD_COLS)
    return pl.pallas_call(
        functools.partial(_rwkv_prep_kernel, one_sequence=one_sequence),
        grid=(m // tm,),
        in_specs=[tile(RW_PAD_COLS), prev_spec, row(RW_PAD_COLS), row(wb), row(wb), row(wb), row(wb),
                  full(LORA_PAD, wb), full(LORA_PAD, wb), full(R_GATE, wb)],
        out_specs=[tile(wb)] * 7,
        out_shape=[jax.ShapeDtypeStruct((m, wb), F32)] * 7,
        compiler_params=_cparams(("parallel",)),
        name="rwkv_prep",
    )(cols, prev_rows, mu, w0, a0, k_k, k_a, w_decay, w_aaa, w_gate)


def _rwkv_pair_chunk(r, lw, k, v, kkr, a, g, lnw, lnb, rk, s_prev):
    cl = CHUNK
    n = D_HEAD_B

    def stack(x):
        x2 = jnp.concatenate([x, x], axis=0)
        row = lax.broadcasted_iota(jnp.int32, x2.shape, 0)
        lane = lax.broadcasted_iota(jnp.int32, x2.shape, 1)
        return jnp.where((row // cl) == (lane // n), x2, 0.0)

    def fold(x_bd):
        return x_bd[0:cl, :] + x_bd[cl:2 * cl, :]

    row_c = lax.broadcasted_iota(jnp.int32, lw.shape, 0)
    cum = lw
    shift = 1
    while shift < cl:
        cum = cum + jnp.where(row_c >= shift, pltpu.roll(cum, shift, 0), 0.0)
        shift *= 2
    cum_last = cum[cl - 1:cl, :]

    kk_bd = stack(kkr)
    norm = jnp.sqrt(jnp.sum(kk_bd * kk_bd, axis=-1, keepdims=True))
    kk = fold(kk_bd * (1.0 / jnp.maximum(norm, 1e-12)))
    b = kk * a

    e_pos = jnp.exp(cum)
    e_neg = jnp.exp(-cum)
    e_rest = jnp.exp(cum_last - cum)
    r_t = stack(r * e_pos)
    a_t = stack(-kk * jnp.exp(cum - lw))
    b_t = b * e_neg
    k_t = k * e_neg
    b_h = stack(b * e_rest)
    k_h = stack(k * e_rest)
    v_bd = stack(v)
    gamma = jnp.exp(cum_last)

    lhs = jnp.concatenate([a_t, r_t], axis=0).astype(BF16)
    rhs = jnp.concatenate([b_t, b_t, k_t, k_t], axis=0).astype(BF16)
    m1 = _dot_nt(lhs, rhs)
    xr = _dot_nt(lhs, s_prev.astype(BF16))
    yield
    c2 = 2 * cl
    row = lax.broadcasted_iota(jnp.int32, (c2, c2), 0)
    col = lax.broadcasted_iota(jnp.int32, (c2, c2), 1)
    same_head = (row // cl) == (col // cl)
    strict = jnp.logical_and(same_head, row > col)
    incl = jnp.logical_and(same_head, row >= col)
    a_ab = jnp.where(strict, m1[0:c2, 0:c2], 0.0)
    a_ak = jnp.where(strict, m1[0:c2, c2:2 * c2], 0.0)
    a_rb = jnp.where(incl, m1[c2:2 * c2, 0:c2], 0.0)
    a_rk = jnp.where(incl, m1[c2:2 * c2, c2:2 * c2], 0.0)

    mm = lambda x, y: _dot(x.astype(BF16), y.astype(BF16))
    eye = jnp.where(row == col, 1.0, 0.0)
    diag_blk = (row // INV_BLOCK) == (col // INV_BLOCK)
    x = jnp.where(diag_blk, a_ab, 0.0)
    a_off = a_ab - x
    t_d = eye + x
    vv = mm(jnp.concatenate([a_ak, a_rk], axis=0), v_bd)
    span = 2
    while span < INV_BLOCK:
        x = mm(x, x)
        yield
        t_d = t_d + mm(t_d, x)
        span *= 2
    yield
    nm = mm(t_d, a_off)
    yield
    nm_pow = nm
    m_acc = eye + nm
    blocks = 2
    while blocks < cl // INV_BLOCK:
        sq = mm(nm_pow, nm_pow)
        yield
        m_acc = m_acc + mm(m_acc, sq)
        yield
        nm_pow = sq
        blocks *= 2
    t_inv = mm(m_acc, t_d)
    yield
    u = mm(t_inv, xr[0:c2, :] + vv[0:c2, :])
    yield
    y_bd = xr[c2:2 * c2, :] + mm(a_rb, u) + vv[c2:2 * c2, :]
    s_new = s_prev * gamma + _dot_tn(jnp.concatenate([u, v_bd], axis=0).astype(BF16),
                                     jnp.concatenate([b_h, k_h], axis=0).astype(BF16))
    yield

    lane2 = lax.broadcasted_iota(jnp.int32, (c2, LANES), 1)
    row2 = lax.broadcasted_iota(jnp.int32, (c2, LANES), 0)
    head_mask = (row2 // cl) == (lane2 // n)
    mean = jnp.sum(y_bd, axis=-1, keepdims=True) * (1.0 / n)
    d = jnp.where(head_mask, y_bd - mean, 0.0)
    var = jnp.sum(d * d, axis=-1, keepdims=True) * (1.0 / n)
    yn = fold(d * lax.rsqrt(var + GN_EPS))
    yn = yn * lnw + lnb
    rk_sum = jnp.sum(stack(r * k * rk), axis=-1, keepdims=True)
    bonus = fold(rk_sum * v_bd)
    return (yn + bonus) * g, s_new


def _run_interleaved(generators):
    results = [None] * len(generators)
    active = list(enumerate(generators))
    while active:
        still = []
        for i, gen in active:
            try:
                next(gen)
                still.append((i, gen))
            except StopIteration as done:
                results[i] = done.value
        active = still
    return results


def _rwkv_chunk_kernel(r_ref, lw_ref, k_ref, v_ref, kkr_ref, a_ref, g_ref, lnw_ref, lnb_ref, rk_ref, s0_ref,
                       y_ref, s_out_ref, s_ref, *, pairs_per_step):
    c = pl.program_id(2)

    @pl.when(c == 0)
    def _():
        s_ref[...] = s0_ref[0]

    in_refs = (r_ref, lw_ref, k_ref, v_ref, kkr_ref, a_ref, g_ref, lnw_ref, lnb_ref, rk_ref)
    lanes = [slice(p * LANES, (p + 1) * LANES) for p in range(pairs_per_step)]
    loaded = [tuple(ref[:, ln] for ref in in_refs) + (s_ref[p],) for p, ln in enumerate(lanes)]
    results = _run_interleaved([_rwkv_pair_chunk(*args) for args in loaded])
    for p, (y, s_new) in enumerate(results):
        y_ref[:, lanes[p]] = y.astype(y_ref.dtype)
        s_ref[p] = s_new

    @pl.when(c == pl.num_programs(2) - 1)
    def _():
        s_out_ref[0] = s_ref[...]


def rwkv_chunked(r, lw, k, v, kkr, a, g, lnx_w, lnx_b, r_k, s0_bd, n_seq, pairs_per_step):
    m = r.shape[0]
    n_pairs = N_HEADS_B // 2
    nc = m // n_seq // CHUNK
    pps = pairs_per_step
    tile = pl.BlockSpec((CHUNK, pps * LANES), lambda s, h, c: (s * nc + c, h))
    par = pl.BlockSpec((1, pps * LANES), lambda s, h, c: (0, h))
    st = pl.BlockSpec((1, pps, LANES, LANES), lambda s, h, c: (s, h, 0, 0))
    return pl.pallas_call(
        functools.partial(_rwkv_chunk_kernel, pairs_per_step=pps),
        grid=(n_seq, n_pairs // pps, nc),
        in_specs=[tile] * 7 + [par] * 3 + [st],
        out_specs=[tile, st],
        out_shape=[jax.ShapeDtypeStruct((m, WIDTH_B), BF16),
                   jax.ShapeDtypeStruct((n_seq, n_pairs, LANES, LANES), F32)],
        scratch_shapes=[pltpu.VMEM((pps, LANES, LANES), F32)],
        compiler_params=_cparams(("parallel", "parallel", "arbitrary")),
        name="rwkv_chunked",
    )(r, lw, k, v, kkr, a, g, lnx_w.reshape(1, -1), lnx_b.reshape(1, -1), r_k.reshape(1, -1), s0_bd)


def _pair_states_to_bd(s):
    bsz = s.shape[0]
    s = s.reshape(bsz, N_HEADS_B // 2, 2, D_HEAD_B, D_HEAD_B)
    z = jnp.zeros_like(s[:, :, 0])
    top = jnp.concatenate([s[:, :, 0], z], axis=-1)
    bot = jnp.concatenate([z, s[:, :, 1]], axis=-1)
    return jnp.concatenate([top, bot], axis=-2)


def _bd_to_pair_states(s_bd):
    bsz = s_bd.shape[0]
    n = D_HEAD_B
    s0 = s_bd[:, :, 0:n, 0:n]
    s1 = s_bd[:, :, n:2 * n, n:2 * n]
    return jnp.stack([s0, s1], axis=2).reshape(bsz, N_HEADS_B, n, n)


def _merge_kernel(att_ref, rwk_ref, ga_ref, gb_ref, wa_ref, wb_ref, o_ref):
    ya = _dot(att_ref[...], wa_ref[...])
    yb = _dot(rwk_ref[...], wb_ref[...])
    o_ref[...] = (_sigmoid(ga_ref[...]) * ya + _sigmoid(gb_ref[...]) * yb).astype(o_ref.dtype)


def merge_branches(att, rwk, gates, wa, wb, tm, tn):
    m = att.shape[0]
    n = wa.shape[1]
    nb = n // tn
    return pl.pallas_call(
        _merge_kernel,
        grid=(m // tm, nb),
        in_specs=[pl.BlockSpec((tm, WIDTH_A), lambda i, j: (i, 0)),
                  pl.BlockSpec((tm, WIDTH_B), lambda i, j: (i, 0)),
                  pl.BlockSpec((tm, tn), lambda i, j: (i, j)),
                  pl.BlockSpec((tm, tn), lambda i, j: (i, j + nb)),
                  pl.BlockSpec((WIDTH_A, tn), lambda i, j: (0, j)),
                  pl.BlockSpec((WIDTH_B, tn), lambda i, j: (0, j))],
        out_specs=pl.BlockSpec((tm, tn), lambda i, j: (i, j)),
        out_shape=jax.ShapeDtypeStruct((m, n), BF16),
        compiler_params=_cparams(("parallel", "arbitrary")),
        name="merge_branches",
    )(att, rwk, gates, gates, wa, wb)


def _mm_norm_res_kernel(*refs, next_norm):
    if next_norm:
        a_ref, w_ref, x_ref, g_ref, g_next_ref, o_ref, h_ref, acc_ref = refs
    else:
        a_ref, w_ref, x_ref, g_ref, o_ref, acc_ref = refs
    kk = pl.program_id(1)

    @pl.when(kk == 0)
    def _():
        acc_ref[...] = jnp.zeros(acc_ref.shape, F32)

    acc_ref[...] += _dot(a_ref[...], w_ref[...])

    @pl.when(kk == pl.num_programs(1) - 1)
    def _():
        y = x_ref[...] + _rms(acc_ref[...], g_ref[...], NORM_EPS)
        o_ref[...] = y
        if next_norm:
            h_ref[...] = _rms(y, g_next_ref[...], NORM_EPS).astype(h_ref.dtype)


def matmul_norm_residual(a, w, x, g, tm, tk, name, g_next=None):
    m, kdim = a.shape
    n = w.shape[1]
    vec = pl.BlockSpec((1, n), lambda i, k: (0, 0))
    rows = pl.BlockSpec((tm, n), lambda i, k: (i, 0))
    next_norm = g_next is not None
    return pl.pallas_call(
        functools.partial(_mm_norm_res_kernel, next_norm=next_norm),
        grid=(m // tm, kdim // tk),
        in_specs=[pl.BlockSpec((tm, tk), lambda i, k: (i, k)),
                  pl.BlockSpec((tk, n), lambda i, k: (k, 0)),
                  rows, vec] + ([vec] if next_norm else []),
        out_specs=[rows, rows] if next_norm else rows,
        out_shape=([jax.ShapeDtypeStruct((m, n), F32), jax.ShapeDtypeStruct((m, n), BF16)] if next_norm
                   else jax.ShapeDtypeStruct((m, n), F32)),
        scratch_shapes=[pltpu.VMEM((tm, n), F32)],
        compiler_params=_cparams(("parallel", "arbitrary")),
        name=name,
    )(a, w, x, g.reshape(1, n), *([g_next.reshape(1, n)] if next_norm else []))


def _gate_up_kernel(h_ref, wg_ref, wu_ref, o_ref, wg16_ref, wu16_ref):
    @pl.when(pl.program_id(1) == 0)
    def _():
        wg16_ref[...] = wg_ref[...].astype(BF16)
        wu16_ref[...] = wu_ref[...].astype(BF16)

    h = h_ref[...]
    gt = _dot(h, wg16_ref[...])
    up = _dot(h, wu16_ref[...])
    o_ref[...] = (gt * _sigmoid(gt) * up).astype(o_ref.dtype)


def ffn_gate_up(h, wg, wu, tm, tn):
    m, kdim = h.shape
    n = wg.shape[1]
    return pl.pallas_call(
        _gate_up_kernel,
        grid=(n // tn, m // tm),
        in_specs=[pl.BlockSpec((tm, kdim), lambda j, i: (i, 0)),
                  pl.BlockSpec((kdim, tn), lambda j, i: (0, j)),
                  pl.BlockSpec((kdim, tn), lambda j, i: (0, j))],
        out_specs=pl.BlockSpec((tm, tn), lambda j, i: (i, j)),
        out_shape=jax.ShapeDtypeStruct((m, n), BF16),
        scratch_shapes=[pltpu.VMEM((kdim, tn), BF16), pltpu.VMEM((kdim, tn), BF16)],
        compiler_params=_cparams(("parallel", "arbitrary")),
        name="ffn_gate_up",
    )(h, wg, wu)


def _pad_lora_cols(x):
    o = 3 * WIDTH_B
    z = jnp.zeros(x.shape[:-1] + (LORA_PAD - R_DECAY,), x.dtype)
    return jnp.concatenate([x[..., :o], x[..., o:o + R_DECAY], z,
                            x[..., o + R_DECAY:o + R_DECAY + R_AAA], z,
                            x[..., o + R_DECAY + R_AAA:]], axis=-1)


def _unpad_lora_cols(x):
    o = 3 * WIDTH_B
    return jnp.concatenate([x[..., :o], x[..., o:o + R_DECAY], x[..., o + LORA_PAD:o + LORA_PAD + R_AAA],
                            x[..., o + 2 * LORA_PAD:]], axis=-1)


def _pad_rows(w, rows):
    return jnp.concatenate([w, jnp.zeros((rows - w.shape[0],) + w.shape[1:], w.dtype)], axis=0)


def _tiles(m):
    return dict(
        tm=min(m, 1024),
        tm_norm=min(m, 512),
        tm_prep=min(m, 256),
        tm_out=min(m, 512),
        tn=1024,
        tn_rw=896,
        tn_ffn=512,
        tk_out=D_MODEL,
        tk_ffn=1408,
        rwkv_pairs=N_HEADS_B // 2,
    )


def _group(x, pos, prev_rows, s0, n_seq, wts, tiles, q_dtype, q_scale, attention_fn):
    tm = tiles["tm"]
    h = rmsnorm_bf16(x, wts["norm_mix_pre"], tiles["tm_norm"])
    tables = rope_tables(pos)
    w_in, tn = wts["w_in"], tiles["tn"]
    (q,) = projection(h, w_in, 0, WIDTH_A, [q_dtype], tm, tn, tables, q_scale, name="proj_q")
    one_sequence = prev_rows is None
    k, *k16 = projection(h, w_in, WIDTH_A, WIDTH_A, [F32, BF16] if one_sequence else [F32], tm, tn, tables,
                         name="proj_k")
    v, *v_t = projection(h, w_in, 2 * WIDTH_A, WIDTH_A, [F32], tm, tn, values_t=one_sequence, name="proj_v")
    (rw,) = projection(h, wts["w_rw"], 0, RW_PAD_COLS, [F32], tm, tiles["tn_rw"], name="proj_rw")
    (gates,) = projection(h, wts["w_gates"], 0, 2 * D_MODEL, [F32], tm, tn, name="proj_gates")

    att = attention_fn(q, k, v, k16, v_t)

    prep = rwkv_prep(rw, prev_rows, wts["mu"], wts["w0"], wts["a0"], wts["k_k"], wts["k_a"],
                     wts["w_decay"], wts["w_aaa"], wts["w_gate"], tiles["tm_prep"])
    m = x.shape[0]
    t_seq = m // n_seq
    if t_seq % CHUNK:
        pad = CHUNK - t_seq % CHUNK
        prep = [jnp.pad(p.reshape(n_seq, t_seq, WIDTH_B), ((0, 0), (0, pad), (0, 0))).reshape(-1, WIDTH_B)
                for p in prep]
    rwk, s_bd = rwkv_chunked(*prep, wts["lnx_w"], wts["lnx_b"], wts["r_k"], _pair_states_to_bd(s0), n_seq,
                             tiles["rwkv_pairs"])
    if t_seq % CHUNK:
        rwk = rwk.reshape(n_seq, -1, WIDTH_B)[:, :t_seq].reshape(m, WIDTH_B)
    s_new = _bd_to_pair_states(s_bd)

    mrg = merge_branches(att, rwk, gates, wts["w_branch_a"], wts["w_branch_b"], tm, tiles["tn"])
    x1, hf = matmul_norm_residual(mrg, wts["w_out"], x, wts["norm_mix_post"], tiles["tm_out"], tiles["tk_out"],
                                  "out_proj", g_next=wts["norm_ffn_pre"])
    act = ffn_gate_up(hf, wts["w_ffn_gate"], wts["w_ffn_up"], tm, tiles["tn_ffn"])
    y = matmul_norm_residual(act, wts["w_ffn_down"], x1, wts["norm_ffn_post"], tiles["tm_out"], tiles["tk_ffn"], "ffn_down")
    return y, k, v, rw, s_new


def kernel(x_prompt, x_sample, cache_k, cache_v, state_rwkv, state_shift, page_table, norm_mix_pre, w_in, lambda_q1, lambda_k1, lambda_q2, lambda_k2, subln_w, rw_mu, w0, w_decay, a0, w_aaa, w_gate_lora, k_k, k_a, r_k, lnx_w, lnx_b, w_branch_a, w_branch_b, w_out, norm_mix_post, norm_ffn_pre, w_ffn_gate, w_ffn_up, w_ffn_down, norm_ffn_post):
    depth = w_in.shape[0]
    bsz, seq = x_prompt.shape[0], x_prompt.shape[1]
    db, ds = x_sample.shape[0], x_sample.shape[1]
    assert depth == 1 and bsz == 1 and ds == 1
    n_pool = cache_k.shape[1]
    n_pages = page_table.shape[1]
    past = n_pages * PAGE_SIZE
    l = 0
    lambda_init = 0.8 - 0.6 * math.exp(-0.3 * l)

    wi = w_in[l]
    row = lambda p: p.reshape(1, -1)
    wts = {
        "norm_mix_pre": norm_mix_pre[l],
        "w_in": wi,
        "w_rw": _pad_lora_cols(wi[:, A_COLS:A_COLS + RW_COLS]),
        "w_gates": wi[:, A_COLS + RW_COLS:],
        "mu": _pad_lora_cols(row(rw_mu[l])),
        "w0": row(w0[l]), "a0": row(a0[l]), "k_k": row(k_k[l]), "k_a": row(k_a[l]),
        "w_decay": _pad_rows(w_decay[l], LORA_PAD).astype(BF16),
        "w_aaa": _pad_rows(w_aaa[l], LORA_PAD).astype(BF16),
        "w_gate": w_gate_lora[l].astype(BF16),
        "lnx_w": lnx_w[l], "lnx_b": lnx_b[l], "r_k": r_k[l],
        "w_branch_a": w_branch_a[l].astype(BF16), "w_branch_b": w_branch_b[l].astype(BF16),
        "w_out": w_out[l].astype(BF16), "norm_mix_post": norm_mix_post[l],
        "norm_ffn_pre": norm_ffn_pre[l],
        "w_ffn_gate": w_ffn_gate[l], "w_ffn_up": w_ffn_up[l],
        "w_ffn_down": w_ffn_down[l].astype(BF16), "norm_ffn_post": norm_ffn_post[l],
    }
    lam_params = [row(p[l]) for p in (lambda_q1, lambda_k1, lambda_q2, lambda_k2)]
    subln = subln_w[l]

    tiles_p = _tiles(seq)

    def prompt_attention(q, k, v, k16, v_t):
        del k, v
        return prompt_diff_attention(q, k16[0], v_t[0], lam_params, subln, lambda_init, 512, 512)

    s0_p = jnp.zeros((bsz, N_HEADS_B, D_HEAD_B, D_HEAD_B), F32)
    y_p, k_p, v_p, rw_p, s_p = _group(x_prompt.reshape(seq, D_MODEL), jnp.arange(seq), None, s0_p, 1,
                                      wts, tiles_p, BF16, LOG2_E * D_HEAD_A ** -0.5, prompt_attention)

    tiles_s = _tiles(db)
    ck = jnp.transpose(cache_k[l], (0, 2, 3, 1)).reshape(n_pool, WIDTH_A, PAGE_SIZE)
    cv = cache_v[l]

    def sample_attention(q, k, v, k16, v_t):
        del k16, v_t
        return sample_diff_attention(q, k, v, ck, cv, page_table, lam_params, subln, lambda_init, 8)

    y_s, k_s, v_s, rw_s, s_s = _group(x_sample.reshape(db, D_MODEL), jnp.full((db,), past, jnp.int32),
                                      _pad_lora_cols(state_shift[l]),
                                      state_rwkv[l], db, wts, tiles_s, F32, D_HEAD_A ** -0.5, sample_attention)

    return (y_p.reshape(bsz, seq, D_MODEL),
            y_s.reshape(db, ds, D_MODEL),
            k_p.reshape(1, bsz, seq, 2 * N_HEADS_A, D_HEAD_A),
            v_p.reshape(1, bsz, seq, N_HEADS_A, 2 * D_HEAD_A),
            s_p.reshape(1, bsz, N_HEADS_B, D_HEAD_B, D_HEAD_B),
            _unpad_lora_cols(rw_p[-1:]).reshape(1, bsz, RW_COLS),
            k_s.reshape(1, db, ds, 2 * N_HEADS_A, D_HEAD_A),
            v_s.reshape(1, db, ds, N_HEADS_A, 2 * D_HEAD_A),
            s_s.reshape(1, db, N_HEADS_B, D_HEAD_B, D_HEAD_B),
            _unpad_lora_cols(rw_s).reshape(1, db, RW_COLS))
```

```python
import functools
import math

import jax
import jax.numpy as jnp
from jax import lax
from jax.experimental import pallas as pl
from jax.experimental.pallas import tpu as pltpu

F32 = jnp.float32
BF16 = jnp.bfloat16

D_MODEL = 2048
N_HEADS_A = 8
D_HEAD_A = 64
WIDTH_A = 2 * N_HEADS_A * D_HEAD_A
N_HEADS_B = 16
D_HEAD_B = 64
WIDTH_B = N_HEADS_B * D_HEAD_B
R_DECAY = 96
R_AAA = 96
R_GATE = 256
ROPE_THETA = 10000.0
NORM_EPS = 1e-6
SUBLN_EPS = 1e-5
GN_EPS = 64e-5
LOG2_E = math.log2(math.e)
PAGE_SIZE = 128
A_COLS = 3 * WIDTH_A
RW_COLS = 3 * WIDTH_B + R_DECAY + R_AAA + R_GATE

LANES = 128
SUBLANES = 8
ONES_ROWS = 16
KV_BLOCKS_IN_FLIGHT = 2
LORA_PAD = 128
RW_PAD_COLS = 3 * WIDTH_B + 2 * LORA_PAD + R_GATE
CHUNK = 64
INV_BLOCK = 16
VMEM_LIMIT = 56 * 1024 * 1024


def _cparams(sem):
    return pltpu.CompilerParams(dimension_semantics=sem, vmem_limit_bytes=VMEM_LIMIT)


def _dot(a, b):
    return jnp.dot(a, b, preferred_element_type=F32)


def _dot_nt(a, b):
    return lax.dot_general(a, b, (((1,), (1,)), ((), ())), preferred_element_type=F32)


def _dot_tn(a, b):
    return lax.dot_general(a, b, (((0,), (0,)), ((), ())), preferred_element_type=F32)


def _sigmoid(x):
    return 1.0 / (1.0 + jnp.exp(-x))


def _rms(x, g, eps):
    return x * lax.rsqrt(jnp.mean(x * x, axis=-1, keepdims=True) + eps) * g


def _norm_kernel(x_ref, g_ref, o_ref):
    o_ref[...] = _rms(x_ref[...], g_ref[...], NORM_EPS).astype(o_ref.dtype)


def rmsnorm_bf16(x, g, tm):
    m, d = x.shape
    return pl.pallas_call(
        _norm_kernel,
        grid=(m // tm,),
        in_specs=[pl.BlockSpec((tm, d), lambda i: (i, 0)),
                  pl.BlockSpec((1, d), lambda i: (0, 0))],
        out_specs=pl.BlockSpec((tm, d), lambda i: (i, 0)),
        out_shape=jax.ShapeDtypeStruct((m, d), BF16),
        compiler_params=_cparams(("parallel",)),
        name="rmsnorm_bf16",
    )(x, g.reshape(1, d))


def _rope_tile(x, cos, sin_signed):
    lane = lax.broadcasted_iota(jnp.int32, x.shape, 1)
    first_half = (lane % D_HEAD_A) < (D_HEAD_A // 2)
    partner = jnp.where(first_half,
                        pltpu.roll(x, LANES - D_HEAD_A // 2, 1),
                        pltpu.roll(x, D_HEAD_A // 2, 1))
    return x * cos + partner * sin_signed


def _proj_kernel(*refs, rope, scale, n_out, values_t):
    if rope:
        a_ref, w_ref, cos_ref, sin_ref = refs[:4]
        outs = refs[4:-1]
    else:
        a_ref, w_ref = refs[:2]
        outs = refs[2:-1]
    w16_ref = refs[-1]

    @pl.when(pl.program_id(1) == 0)
    def _():
        w16_ref[...] = w_ref[...].astype(BF16)

    acc = _dot_nt(a_ref[...], w16_ref[...])
    if rope:
        cos = cos_ref[...]
        sin = sin_ref[...]
        tn = acc.shape[1]
        acc = jnp.concatenate(
            [_rope_tile(acc[:, c * LANES:(c + 1) * LANES], cos, sin) for c in range(tn // LANES)], axis=1)
    if scale != 1.0:
        acc = acc * scale
    for o_ref in outs[:n_out]:
        o_ref[...] = acc.astype(o_ref.dtype)
    if values_t:
        t_ref = outs[n_out]
        acc_t = jnp.transpose(acc)
        for grp in range(acc.shape[1] // LANES):
            t_ref[grp, 0:LANES, :] = acc_t[grp * LANES:(grp + 1) * LANES, :].astype(t_ref.dtype)
            t_ref[grp, LANES:, :] = jnp.ones((ONES_ROWS, acc.shape[0]), t_ref.dtype)


def projection(a, w, col0, n, out_dtypes, tm, tn, rope_tables=None, scale=1.0, values_t=False,
               name="projection"):
    m, kdim = a.shape
    j0 = col0 // tn
    in_specs = [pl.BlockSpec((tm, kdim), lambda j, i: (i, 0)),
                pl.BlockSpec((tn, kdim), lambda j, i: (j + j0, 0))]
    args = [a, w]
    if rope_tables is not None:
        in_specs += [pl.BlockSpec((tm, LANES), lambda j, i: (i, 0))] * 2
        args += list(rope_tables)
    out_specs = [pl.BlockSpec((tm, tn), lambda j, i: (i, j)) for _ in out_dtypes]
    out_shape = [jax.ShapeDtypeStruct((m, n), dt) for dt in out_dtypes]
    if values_t:
        rows = LANES + ONES_ROWS
        out_specs.append(pl.BlockSpec((tn // LANES, rows, tm), lambda j, i: (j, 0, i)))
        out_shape.append(jax.ShapeDtypeStruct((n // LANES, rows, m), BF16))
    outs = pl.pallas_call(
        functools.partial(_proj_kernel, rope=rope_tables is not None, scale=scale, n_out=len(out_dtypes),
                          values_t=values_t),
        grid=(n // tn, m // tm),
        in_specs=in_specs,
        out_specs=out_specs,
        out_shape=out_shape,
        scratch_shapes=[pltpu.VMEM((tn, kdim), BF16)],
        compiler_params=_cparams(("parallel", "arbitrary")),
        name=name,
    )(*args)
    return outs


def rope_tables(pos):
    half = D_HEAD_A // 2
    inv = 1.0 / (ROPE_THETA ** (jnp.arange(half, dtype=F32) / half))
    ang = pos.astype(F32)[:, None] * inv[None, :]
    cos = jnp.cos(ang)
    sin = jnp.sin(ang)
    cos2 = jnp.concatenate([cos, cos, cos, cos], axis=1)
    sin2 = jnp.concatenate([-sin, sin, -sin, sin], axis=1)
    return cos2, sin2


def _diff_lambda(lq1, lk1, lq2, lk2, lambda_init):
    s1 = jnp.sum(lq1 * lk1, axis=-1, keepdims=True)
    s2 = jnp.sum(lq2 * lk2, axis=-1, keepdims=True)
    return jnp.exp(s1) - jnp.exp(s2) + lambda_init


def _flash_kernel(q_ref, k_ref, v_ref, lq1_ref, lk1_ref, lq2_ref, lk2_ref, g_ref, o_ref,
                  qs_ref, m_ref, acc_ref, s_ref, p_ref, *, tq, tk, col_chunk, row_chunk, lambda_init):
    qi = pl.program_id(1)
    ratio = tq // tk
    qt = jnp.transpose(q_ref[...].astype(F32))
    qt2 = jnp.concatenate([qt, qt], axis=1)
    row = lax.broadcasted_iota(jnp.int32, qt2.shape, 0)
    col = lax.broadcasted_iota(jnp.int32, qt2.shape, 1)
    qs_ref[...] = jnp.where((row < D_HEAD_A) == (col < tq), qt2, 0.0).astype(BF16)
    m_ref[...] = jnp.full(m_ref.shape, -jnp.inf, F32)
    acc_ref[...] = jnp.zeros(acc_ref.shape, F32)

    def column_chunk(c, slot, k_blk, v_t, kv, masked):
        cols = slice(c * col_chunk, (c + 1) * col_chunk)
        s_ref[slot, :, cols] = _dot(k_blk, qs_ref[:, cols])
        yield
        sub = SUBLANES
        groups = row_chunk // sub

        def scores(r0):
            blk = s_ref[slot, r0:r0 + row_chunk, cols]
            if masked:
                krow = lax.broadcasted_iota(jnp.int32, blk.shape, 0) + r0
                qcol = lax.broadcasted_iota(jnp.int32, blk.shape, 1) + (c * col_chunk) % tq
                blk = jnp.where(kv * tk + krow <= qi * tq + qcol, blk, -jnp.inf)
            return blk

        part = None
        for r0 in range(0, tk, row_chunk):
            mx = jnp.max(scores(r0).reshape(groups, sub, col_chunk), axis=0)
            part = mx if part is None else jnp.maximum(part, mx)
        m_prev = m_ref[:, cols]
        m_new = jnp.maximum(m_prev, jnp.max(part, axis=0, keepdims=True))
        alpha = jnp.exp2(m_prev - m_new)
        for r0 in range(0, tk, row_chunk):
            p_ref[slot, r0:r0 + row_chunk, cols] = jnp.exp2(scores(r0) - m_new).astype(BF16)
        m_ref[:, cols] = m_new
        pv = _dot(v_t, p_ref[slot, :, cols])
        yield
        acc_ref[:, cols] = alpha * acc_ref[:, cols] + pv

    def blocks(kvs, masked):
        chunks = []
        for slot, kv in enumerate(kvs):
            start = pl.multiple_of(kv * tk, tk)
            k_blk = k_ref[pl.ds(start, tk), :]
            v_t = v_ref[0, :, pl.ds(start, tk)]
            chunks += [column_chunk(c, slot, k_blk, v_t, kv, masked) for c in range(2 * tq // col_chunk)]
        _run_interleaved(chunks)

    n_full = qi * ratio
    slots = s_ref.shape[0]

    def full_blocks(j, carry):
        blocks([j * slots + s for s in range(slots)], False)
        return carry

    lax.fori_loop(0, n_full // slots, full_blocks, 0)
    for rem in range(1, slots):
        @pl.when(n_full % slots == rem)
        def _(rem=rem):
            blocks([n_full - rem + s for s in range(rem)], False)
    blocks([n_full + r for r in range(ratio)], True)

    lam = _diff_lambda(lq1_ref[...], lk1_ref[...], lq2_ref[...], lk2_ref[...], lambda_init)
    o = acc_ref[0:LANES, :] / acc_ref[LANES:LANES + 1, :]
    o = jnp.transpose(o[:, 0:tq] - lam * o[:, tq:2 * tq])
    o = _rms(o, g_ref[...], SUBLN_EPS) * (1.0 - lambda_init)
    o_ref[...] = o.astype(o_ref.dtype)


def prompt_diff_attention(q, k, v_t, lam_params, subln_w, lambda_init, tq, tk):
    t = q.shape[0]
    n_pairs = N_HEADS_A
    small = pl.BlockSpec((1, D_HEAD_A), lambda h, i: (0, 0))
    whole = pl.BlockSpec((t, LANES), lambda h, i: (0, h))
    whole_t = pl.BlockSpec((1, LANES + ONES_ROWS, t), lambda h, i: (h, 0, 0))
    return pl.pallas_call(
        functools.partial(_flash_kernel, tq=tq, tk=tk, col_chunk=2 * LANES, row_chunk=2 * SUBLANES,
                          lambda_init=lambda_init),
        grid=(n_pairs, t // tq),
        in_specs=[pl.BlockSpec((tq, LANES), lambda h, i: (i, h)),
                  whole, whole_t,
                  small, small, small, small,
                  pl.BlockSpec((1, LANES), lambda h, i: (0, 0))],
        out_specs=pl.BlockSpec((tq, LANES), lambda h, i: (i, h)),
        out_shape=jax.ShapeDtypeStruct((t, WIDTH_A), BF16),
        scratch_shapes=[pltpu.VMEM((LANES, 2 * tq), BF16),
                        pltpu.VMEM((1, 2 * tq), F32),
                        pltpu.VMEM((LANES + ONES_ROWS, 2 * tq), F32),
                        pltpu.VMEM((KV_BLOCKS_IN_FLIGHT, tk, 2 * tq), F32),
                        pltpu.VMEM((KV_BLOCKS_IN_FLIGHT, tk, 2 * tq), BF16)],
        compiler_params=_cparams(("parallel", "arbitrary")),
        name="prompt_diff_attention",
    )(q, k, v_t, *lam_params, subln_w.reshape(1, LANES))


def _decode_kernel(pt_ref, q_ref, kn_ref, vn_ref, lq1_ref, lk1_ref, lq2_ref, lk2_ref, g_ref, *rest,
                   pages_per_step, lambda_init):
    del pt_ref
    k_refs = rest[:pages_per_step]
    v_refs = rest[pages_per_step:2 * pages_per_step]
    o_ref = rest[2 * pages_per_step]
    qm_ref, ex_ref, m_ref, l_ref, acc_ref = rest[2 * pages_per_step + 1:]
    hp = N_HEADS_A
    n_rows = 2 * hp
    j = pl.program_id(1)

    def pair_mask():
        row = lax.broadcasted_iota(jnp.int32, (n_rows, PAGE_SIZE * hp), 0)
        lane = lax.broadcasted_iota(jnp.int32, (n_rows, PAGE_SIZE * hp), 1)
        return (lane % hp) == (row % hp)

    @pl.when(j == 0)
    def _():
        row = lax.broadcasted_iota(jnp.int32, (n_rows, WIDTH_A), 0)
        lane = lax.broadcasted_iota(jnp.int32, (n_rows, WIDTH_A), 1)
        qb = jnp.broadcast_to(q_ref[0], (n_rows, WIDTH_A))
        qm_ref[...] = jnp.where(lane // D_HEAD_A == 2 * (row % hp) + row // hp, qb, 0.0)
        tok = lax.broadcasted_iota(jnp.int32, ex_ref.shape, 0)
        pos = lax.broadcasted_iota(jnp.int32, ex_ref.shape, 1)
        ex_ref[...] = jnp.where(pos // hp == tok, 1.0, 0.0).astype(BF16)
        m_ref[...] = jnp.full(m_ref.shape, -jnp.inf, F32)
        l_ref[...] = jnp.zeros(l_ref.shape, F32)
        acc_ref[...] = jnp.zeros(acc_ref.shape, F32)

    qm = qm_ref[...].astype(BF16)
    keep = pair_mask()
    pages = range(pages_per_step)
    s = jnp.concatenate([_dot(qm, k_refs[i][0].astype(BF16)) for i in pages], axis=1)
    m_prev = m_ref[...]
    m_new = jnp.maximum(m_prev, jnp.max(s, axis=-1, keepdims=True))
    alpha = jnp.exp(m_prev - m_new)
    p = jnp.exp(s - m_new)
    l_ref[...] = alpha * l_ref[...] + jnp.sum(p, axis=-1, keepdims=True)
    p16 = p.astype(BF16)
    p_wide = [jnp.where(keep, _dot(p16[:, i * PAGE_SIZE:(i + 1) * PAGE_SIZE], ex_ref[...]), 0.0).astype(BF16)
              for i in pages]
    pv = _dot(p_wide[0], v_refs[0][0].reshape(PAGE_SIZE * hp, LANES).astype(BF16))
    for i in pages[1:]:
        pv += _dot(p_wide[i], v_refs[i][0].reshape(PAGE_SIZE * hp, LANES).astype(BF16))
    acc_ref[...] = alpha * acc_ref[...] + pv
    m_ref[...] = m_new

    @pl.when(j == pl.num_programs(1) - 1)
    def _():
        s = jnp.sum(qm_ref[...] * kn_ref[0], axis=-1, keepdims=True)
        m_prev = m_ref[...]
        m_new = jnp.maximum(m_prev, s)
        alpha = jnp.exp(m_prev - m_new)
        p = jnp.exp(s - m_new)
        l = alpha * l_ref[...] + p
        vn = vn_ref[0]
        acc = alpha * acc_ref[...] + p * jnp.concatenate([vn, vn], axis=0)
        lam = _diff_lambda(lq1_ref[...], lk1_ref[...], lq2_ref[...], lk2_ref[...], lambda_init)
        o = acc / l
        o = o[0:hp, :] - lam * o[hp:n_rows, :]
        o_ref[0] = (_rms(o, g_ref[...], SUBLN_EPS) * (1.0 - lambda_init)).astype(o_ref.dtype)


def sample_diff_attention(q, k_new, v_new, cache_k, cache_v, page_table, lam_params, subln_w,
                          lambda_init, pages_per_step):
    b = q.shape[0]
    n_pages = page_table.shape[1]
    steps = n_pages // pages_per_step
    hp = N_HEADS_A
    row3 = lambda x: x.reshape(b, 1, WIDTH_A)
    vec = pl.BlockSpec((1, 1, WIDTH_A), lambda bi, j, pt: (bi, 0, 0))
    small = pl.BlockSpec((1, D_HEAD_A), lambda bi, j, pt: (0, 0))
    pairs = pl.BlockSpec((1, hp, LANES), lambda bi, j, pt: (bi, 0, 0))

    def page_spec(i, shape):
        zeros = (0,) * len(shape)
        return pl.BlockSpec((1,) + shape,
                            lambda bi, j, pt: (pt[bi * n_pages + j * pages_per_step + i],) + zeros)

    grid_spec = pltpu.PrefetchScalarGridSpec(
        num_scalar_prefetch=1,
        grid=(b, steps),
        in_specs=[vec, vec, pairs, small, small, small, small,
                  pl.BlockSpec((1, LANES), lambda bi, j, pt: (0, 0))]
                 + [page_spec(i, (WIDTH_A, PAGE_SIZE)) for i in range(pages_per_step)]
                 + [page_spec(i, (PAGE_SIZE, hp, LANES)) for i in range(pages_per_step)],
        out_specs=pairs,
        scratch_shapes=[pltpu.VMEM((2 * hp, WIDTH_A), F32),
                        pltpu.VMEM((PAGE_SIZE, PAGE_SIZE * hp), BF16),
                        pltpu.VMEM((2 * hp, 1), F32),
                        pltpu.VMEM((2 * hp, 1), F32),
                        pltpu.VMEM((2 * hp, LANES), F32)],
    )
    out = pl.pallas_call(
        functools.partial(_decode_kernel, pages_per_step=pages_per_step, lambda_init=lambda_init),
        grid_spec=grid_spec,
        out_shape=jax.ShapeDtypeStruct((b, hp, LANES), BF16),
        compiler_params=_cparams(("parallel", "arbitrary")),
        name="sample_diff_attention",
    )(page_table.reshape(-1), row3(q), row3(k_new), v_new.reshape(b, hp, LANES), *lam_params,
      subln_w.reshape(1, LANES), *([cache_k] * pages_per_step), *([cache_v] * pages_per_step))
    return out.reshape(b, WIDTH_A)


def _rwkv_prep_kernel(c_ref, p_ref, mu_ref, w0_ref, a0_ref, kk_ref, ka_ref, wdec_ref, waaa_ref, wgate_ref,
                      r_ref, lw_ref, k_ref, v_ref, kkr_ref, a_ref, g_ref, *, one_sequence):
    c = c_ref[...]
    if one_sequence:
        above = jnp.where(pl.program_id(0) == 0, 0.0, p_ref[SUBLANES - 1:SUBLANES, :])
        row = lax.broadcasted_iota(jnp.int32, c.shape, 0)
        prev = jnp.where(row == 0, above, pltpu.roll(c, 1, 0))
    else:
        prev = p_ref[...]
    u = c + mu_ref[...] * (prev - c)
    wb = WIDTH_B
    r = u[:, 0:wb]
    kx = u[:, wb:2 * wb]
    v = u[:, 2 * wb:3 * wb]
    o = 3 * wb
    wd = u[:, o:o + LORA_PAD]
    ad = u[:, o + LORA_PAD:o + 2 * LORA_PAD]
    gd = u[:, o + 2 * LORA_PAD:]
    z = w0_ref[...] + _dot(jnp.tanh(wd).astype(BF16), wdec_ref[...])
    nz = -z
    softplus = jnp.maximum(nz, 0.0) + jnp.log(1.0 + jnp.exp(-jnp.abs(nz)))
    w_raw = -softplus - 0.5
    a = _sigmoid(a0_ref[...] + _dot(ad.astype(BF16), waaa_ref[...]))
    r_ref[...] = r
    lw_ref[...] = -jnp.exp(w_raw)
    k_ref[...] = kx * (1.0 + (a - 1.0) * ka_ref[...])
    v_ref[...] = v
    kkr_ref[...] = kx * kk_ref[...]
    a_ref[...] = a
    g_ref[...] = _dot(_sigmoid(gd).astype(BF16), wgate_ref[...])


def rwkv_prep(cols, prev_rows, mu, w0, a0, k_k, k_a, w_decay, w_aaa, w_gate, tm):
    m = cols.shape[0]
    wb = WIDTH_B
    row = lambda n: pl.BlockSpec((1, n), lambda i: (0, 0))
    full = lambda r_, c_: pl.BlockSpec((r_, c_), lambda i: (0, 0))
    tile = lambda n: pl.BlockSpec((tm, n), lambda i: (i, 0))
    one_sequence = prev_rows is None
    if one_sequence:
        prev_rows = cols
        prev_spec = pl.BlockSpec((SUBLANES, RW_PAD_COLS),
                                 lambda i: (jnp.maximum(i * (tm // SUBLANES) - 1, 0), 0))
    else:
        prev_spec = tile(RW_PAD_COLS)
    return pl.pallas_call(
        functools.partial(_rwkv_prep_kernel, one_sequence=one_sequence),
        grid=(m // tm,),
        in_specs=[tile(RW_PAD_COLS), prev_spec, row(RW_PAD_COLS), row(wb), row(wb), row(wb), row(wb),
                  full(LORA_PAD, wb), full(LORA_PAD, wb), full(R_GATE, wb)],
        out_specs=[tile(wb)] * 7,
        out_shape=[jax.ShapeDtypeStruct((m, wb), F32)] * 7,
        compiler_params=_cparams(("parallel",)),
        name="rwkv_prep",
    )(cols, prev_rows, mu, w0, a0, k_k, k_a, w_decay, w_aaa, w_gate)


def _rwkv_pair_chunk(r, lw, k, v, kkr, a, g, lnw, lnb, rk, s_prev):
    cl = CHUNK
    n = D_HEAD_B

    def stack(x):
        x2 = jnp.concatenate([x, x], axis=0)
        row = lax.broadcasted_iota(jnp.int32, x2.shape, 0)
        lane = lax.broadcasted_iota(jnp.int32, x2.shape, 1)
        return jnp.where((row // cl) == (lane // n), x2, 0.0)

    def fold(x_bd):
        return x_bd[0:cl, :] + x_bd[cl:2 * cl, :]

    row_c = lax.broadcasted_iota(jnp.int32, lw.shape, 0)
    cum = lw
    shift = 1
    while shift < cl:
        cum = cum + jnp.where(row_c >= shift, pltpu.roll(cum, shift, 0), 0.0)
        shift *= 2
    cum_last = cum[cl - 1:cl, :]

    kk_bd = stack(kkr)
    norm = jnp.sqrt(jnp.sum(kk_bd * kk_bd, axis=-1, keepdims=True))
    kk = fold(kk_bd * (1.0 / jnp.maximum(norm, 1e-12)))
    b = kk * a

    e_pos = jnp.exp(cum)
    e_neg = jnp.exp(-cum)
    e_rest = jnp.exp(cum_last - cum)
    r_t = stack(r * e_pos)
    a_t = stack(-kk * jnp.exp(cum - lw))
    b_t = b * e_neg
    k_t = k * e_neg
    b_h = stack(b * e_rest)
    k_h = stack(k * e_rest)
    v_bd = stack(v)
    gamma = jnp.exp(cum_last)

    lhs = jnp.concatenate([a_t, r_t], axis=0).astype(BF16)
    rhs = jnp.concatenate([b_t, b_t, k_t, k_t], axis=0).astype(BF16)
    m1 = _dot_nt(lhs, rhs)
    xr = _dot_nt(lhs, s_prev.astype(BF16))
    yield
    c2 = 2 * cl
    row = lax.broadcasted_iota(jnp.int32, (c2, c2), 0)
    col = lax.broadcasted_iota(jnp.int32, (c2, c2), 1)
    same_head = (row // cl) == (col // cl)
    strict = jnp.logical_and(same_head, row > col)
    incl = jnp.logical_and(same_head, row >= col)
    a_ab = jnp.where(strict, m1[0:c2, 0:c2], 0.0)
    a_ak = jnp.where(strict, m1[0:c2, c2:2 * c2], 0.0)
    a_rb = jnp.where(incl, m1[c2:2 * c2, 0:c2], 0.0)
    a_rk = jnp.where(incl, m1[c2:2 * c2, c2:2 * c2], 0.0)

    mm = lambda x, y: _dot(x.astype(BF16), y.astype(BF16))
    eye = jnp.where(row == col, 1.0, 0.0)
    diag_blk = (row // INV_BLOCK) == (col // INV_BLOCK)
    x = jnp.where(diag_blk, a_ab, 0.0)
    a_off = a_ab - x
    t_d = eye + x
    vv = mm(jnp.concatenate([a_ak, a_rk], axis=0), v_bd)
    span = 2
    while span < INV_BLOCK:
        x = mm(x, x)
        yield
        t_d = t_d + mm(t_d, x)
        span *= 2
    yield
    nm = mm(t_d, a_off)
    yield
    nm_pow = nm
    m_acc = eye + nm
    blocks = 2
    while blocks < cl // INV_BLOCK:
        sq = mm(nm_pow, nm_pow)
        yield
        m_acc = m_acc + mm(m_acc, sq)
        yield
        nm_pow = sq
        blocks *= 2
    t_inv = mm(m_acc, t_d)
    yield
    u = mm(t_inv, xr[0:c2, :] + vv[0:c2, :])
    yield
    y_bd = xr[c2:2 * c2, :] + mm(a_rb, u) + vv[c2:2 * c2, :]
    s_new = s_prev * gamma + _dot_tn(jnp.concatenate([u, v_bd], axis=0).astype(BF16),
                                     jnp.concatenate([b_h, k_h], axis=0).astype(BF16))
    yield

    lane2 = lax.broadcasted_iota(jnp.int32, (c2, LANES), 1)
    row2 = lax.broadcasted_iota(jnp.int32, (c2, LANES), 0)
    head_mask = (row2 // cl) == (lane2 // n)
    mean = jnp.sum(y_bd, axis=-1, keepdims=True) * (1.0 / n)
    d = jnp.where(head_mask, y_bd - mean, 0.0)
    var = jnp.sum(d * d, axis=-1, keepdims=True) * (1.0 / n)
    yn = fold(d * lax.rsqrt(var + GN_EPS))
    yn = yn * lnw + lnb
    rk_sum = jnp.sum(stack(r * k * rk), axis=-1, keepdims=True)
    bonus = fold(rk_sum * v_bd)
    return (yn + bonus) * g, s_new


def _run_interleaved(generators):
    results = [None] * len(generators)
    active = list(enumerate(generators))
    while active:
        still = []
        for i, gen in active:
            try:
                next(gen)
                still.append((i, gen))
            except StopIteration as done:
                results[i] = done.value
        active = still
    return results


def _rwkv_chunk_kernel(r_ref, lw_ref, k_ref, v_ref, kkr_ref, a_ref, g_ref, lnw_ref, lnb_ref, rk_ref, s0_ref,
                       y_ref, s_out_ref, s_ref, *, pairs_per_step):
    c = pl.program_id(2)

    @pl.when(c == 0)
    def _():
        s_ref[...] = s0_ref[0]

    in_refs = (r_ref, lw_ref, k_ref, v_ref, kkr_ref, a_ref, g_ref, lnw_ref, lnb_ref, rk_ref)
    lanes = [slice(p * LANES, (p + 1) * LANES) for p in range(pairs_per_step)]
    loaded = [tuple(ref[:, ln] for ref in in_refs) + (s_ref[p],) for p, ln in enumerate(lanes)]
    results = _run_interleaved([_rwkv_pair_chunk(*args) for args in loaded])
    for p, (y, s_new) in enumerate(results):
        y_ref[:, lanes[p]] = y.astype(y_ref.dtype)
        s_ref[p] = s_new

    @pl.when(c == pl.num_programs(2) - 1)
    def _():
        s_out_ref[0] = s_ref[...]


def rwkv_chunked(r, lw, k, v, kkr, a, g, lnx_w, lnx_b, r_k, s0_bd, n_seq, pairs_per_step):
    m = r.shape[0]
    n_pairs = N_HEADS_B // 2
    nc = m // n_seq // CHUNK
    pps = pairs_per_step
    tile = pl.BlockSpec((CHUNK, pps * LANES), lambda s, h, c: (s * nc + c, h))
    par = pl.BlockSpec((1, pps * LANES), lambda s, h, c: (0, h))
    st = pl.BlockSpec((1, pps, LANES, LANES), lambda s, h, c: (s, h, 0, 0))
    return pl.pallas_call(
        functools.partial(_rwkv_chunk_kernel, pairs_per_step=pps),
        grid=(n_seq, n_pairs // pps, nc),
        in_specs=[tile] * 7 + [par] * 3 + [st],
        out_specs=[tile, st],
        out_shape=[jax.ShapeDtypeStruct((m, WIDTH_B), BF16),
                   jax.ShapeDtypeStruct((n_seq, n_pairs, LANES, LANES), F32)],
        scratch_shapes=[pltpu.VMEM((pps, LANES, LANES), F32)],
        compiler_params=_cparams(("parallel", "parallel", "arbitrary")),
        name="rwkv_chunked",
    )(r, lw, k, v, kkr, a, g, lnx_w.reshape(1, -1), lnx_b.reshape(1, -1), r_k.reshape(1, -1), s0_bd)


def _rwkv_step_kernel(r_ref, lw_ref, k_ref, v_ref, kkr_ref, a_ref, g_ref, lnw_ref, lnb_ref, rk_ref, s_ref,
                      y_ref, s_out_ref):
    n = D_HEAD_B
    seqs = r_ref.shape[0]
    lane = lax.broadcasted_iota(jnp.int32, (seqs, LANES), 1)
    first = lane < n
    sq_row = lax.broadcasted_iota(jnp.int32, (LANES, LANES), 0)
    sq_lane = lax.broadcasted_iota(jnp.int32, (LANES, LANES), 1)
    same_head = (sq_row // n) == (sq_lane // n)

    def head_sums(x):
        lo = jnp.sum(jnp.where(first, x, 0.0), axis=-1, keepdims=True)
        hi = jnp.sum(jnp.where(first, 0.0, x), axis=-1, keepdims=True)
        return jnp.where(first, lo, hi)

    def to_columns(x):
        return jnp.transpose(jnp.concatenate([x, jnp.zeros((LANES - seqs, LANES), F32)], axis=0))

    for p in range(s_ref.shape[1]):
        ln = slice(p * LANES, (p + 1) * LANES)
        r, k, v, a = r_ref[:, ln], k_ref[:, ln], v_ref[:, ln], a_ref[:, ln]
        kkr = kkr_ref[:, ln]
        kk = kkr * (1.0 / jnp.maximum(jnp.sqrt(head_sums(kkr * kkr)), 1e-12))
        b = kk * a
        decay = jnp.exp(lw_ref[:, ln])
        v_cols = to_columns(v)
        y_cols = jnp.zeros((LANES, LANES), F32)
        for s in range(seqs):
            row = slice(s, s + 1)
            state = s_ref[s, p]
            sa = jnp.sum(state * (-kk[row]), axis=-1, keepdims=True)
            update = sa * b[row] + v_cols[:, s:s + 1] * k[row]
            state = state * decay[row] + jnp.where(same_head, update, 0.0)
            s_out_ref[s, p] = state
            y_col = jnp.sum(state * r[row], axis=-1, keepdims=True)
            y_cols = jnp.where(sq_lane == s, y_col, y_cols)
        y = jnp.transpose(y_cols)[0:seqs, :]
        mean = head_sums(y) * (1.0 / n)
        d = y - mean
        var = head_sums(d * d) * (1.0 / n)
        yn = d * lax.rsqrt(var + GN_EPS) * lnw_ref[:, ln] + lnb_ref[:, ln]
        bonus = head_sums(r * k * rk_ref[:, ln]) * v
        y_ref[:, ln] = ((yn + bonus) * g_ref[:, ln]).astype(y_ref.dtype)


def rwkv_step(r, lw, k, v, kkr, a, g, lnx_w, lnx_b, r_k, s0_bd, seqs_per_step):
    n_seq = r.shape[0]
    n_pairs = N_HEADS_B // 2
    tile = pl.BlockSpec((seqs_per_step, WIDTH_B), lambda i: (i, 0))
    par = pl.BlockSpec((1, WIDTH_B), lambda i: (0, 0))
    st = pl.BlockSpec((seqs_per_step, n_pairs, LANES, LANES), lambda i: (i, 0, 0, 0))
    return pl.pallas_call(
        _rwkv_step_kernel,
        grid=(n_seq // seqs_per_step,),
        in_specs=[tile] * 7 + [par] * 3 + [st],
        out_specs=[tile, st],
        out_shape=[jax.ShapeDtypeStruct((n_seq, WIDTH_B), BF16),
                   jax.ShapeDtypeStruct((n_seq, n_pairs, LANES, LANES), F32)],
        compiler_params=_cparams(("parallel",)),
        name="rwkv_step",
    )(r, lw, k, v, kkr, a, g, lnx_w.reshape(1, -1), lnx_b.reshape(1, -1), r_k.reshape(1, -1), s0_bd)


def _pair_states_to_bd(s):
    bsz = s.shape[0]
    s = s.reshape(bsz, N_HEADS_B // 2, 2, D_HEAD_B, D_HEAD_B)
    z = jnp.zeros_like(s[:, :, 0])
    top = jnp.concatenate([s[:, :, 0], z], axis=-1)
    bot = jnp.concatenate([z, s[:, :, 1]], axis=-1)
    return jnp.concatenate([top, bot], axis=-2)


def _bd_to_pair_states(s_bd):
    bsz = s_bd.shape[0]
    n = D_HEAD_B
    s0 = s_bd[:, :, 0:n, 0:n]
    s1 = s_bd[:, :, n:2 * n, n:2 * n]
    return jnp.stack([s0, s1], axis=2).reshape(bsz, N_HEADS_B, n, n)


def _merge_kernel(att_ref, rwk_ref, ga_ref, gb_ref, wa_ref, wb_ref, o_ref):
    ya = _dot(att_ref[...], wa_ref[...])
    yb = _dot(rwk_ref[...], wb_ref[...])
    o_ref[...] = (_sigmoid(ga_ref[...]) * ya + _sigmoid(gb_ref[...]) * yb).astype(o_ref.dtype)


def merge_branches(att, rwk, gates, wa, wb, tm, tn):
    m = att.shape[0]
    n = wa.shape[1]
    nb = n // tn
    return pl.pallas_call(
        _merge_kernel,
        grid=(m // tm, nb),
        in_specs=[pl.BlockSpec((tm, WIDTH_A), lambda i, j: (i, 0)),
                  pl.BlockSpec((tm, WIDTH_B), lambda i, j: (i, 0)),
                  pl.BlockSpec((tm, tn), lambda i, j: (i, j)),
                  pl.BlockSpec((tm, tn), lambda i, j: (i, j + nb)),
                  pl.BlockSpec((WIDTH_A, tn), lambda i, j: (0, j)),
                  pl.BlockSpec((WIDTH_B, tn), lambda i, j: (0, j))],
        out_specs=pl.BlockSpec((tm, tn), lambda i, j: (i, j)),
        out_shape=jax.ShapeDtypeStruct((m, n), BF16),
        compiler_params=_cparams(("parallel", "arbitrary")),
        name="merge_branches",
    )(att, rwk, gates, gates, wa, wb)


def _mm_norm_res_kernel(*refs, next_norm):
    if next_norm:
        a_ref, w_ref, x_ref, g_ref, g_next_ref, o_ref, h_ref, acc_ref = refs
    else:
        a_ref, w_ref, x_ref, g_ref, o_ref, acc_ref = refs
    kk = pl.program_id(1)

    @pl.when(kk == 0)
    def _():
        acc_ref[...] = jnp.zeros(acc_ref.shape, F32)

    acc_ref[...] += _dot(a_ref[...], w_ref[...])

    @pl.when(kk == pl.num_programs(1) - 1)
    def _():
        y = x_ref[...] + _rms(acc_ref[...], g_ref[...], NORM_EPS)
        o_ref[...] = y
        if next_norm:
            h_ref[...] = _rms(y, g_next_ref[...], NORM_EPS).astype(h_ref.dtype)


def matmul_norm_residual(a, w, x, g, tm, tk, name, g_next=None):
    m, kdim = a.shape
    n = w.shape[1]
    vec = pl.BlockSpec((1, n), lambda i, k: (0, 0))
    rows = pl.BlockSpec((tm, n), lambda i, k: (i, 0))
    next_norm = g_next is not None
    return pl.pallas_call(
        functools.partial(_mm_norm_res_kernel, next_norm=next_norm),
        grid=(m // tm, kdim // tk),
        in_specs=[pl.BlockSpec((tm, tk), lambda i, k: (i, k)),
                  pl.BlockSpec((tk, n), lambda i, k: (k, 0)),
                  rows, vec] + ([vec] if next_norm else []),
        out_specs=[rows, rows] if next_norm else rows,
        out_shape=([jax.ShapeDtypeStruct((m, n), F32), jax.ShapeDtypeStruct((m, n), BF16)] if next_norm
                   else jax.ShapeDtypeStruct((m, n), F32)),
        scratch_shapes=[pltpu.VMEM((tm, n), F32)],
        compiler_params=_cparams(("parallel", "arbitrary")),
        name=name,
    )(a, w, x, g.reshape(1, n), *([g_next.reshape(1, n)] if next_norm else []))


def _gate_up_kernel(h_ref, wg_ref, wu_ref, o_ref, wg16_ref, wu16_ref):
    @pl.when(pl.program_id(1) == 0)
    def _():
        wg16_ref[...] = wg_ref[...].astype(BF16)
        wu16_ref[...] = wu_ref[...].astype(BF16)

    h = h_ref[...]
    gt = _dot(h, wg16_ref[...])
    up = _dot(h, wu16_ref[...])
    o_ref[...] = (gt * _sigmoid(gt) * up).astype(o_ref.dtype)


def ffn_gate_up(h, wg, wu, tm, tn):
    m, kdim = h.shape
    n = wg.shape[1]
    return pl.pallas_call(
        _gate_up_kernel,
        grid=(n // tn, m // tm),
        in_specs=[pl.BlockSpec((tm, kdim), lambda j, i: (i, 0)),
                  pl.BlockSpec((kdim, tn), lambda j, i: (0, j)),
                  pl.BlockSpec((kdim, tn), lambda j, i: (0, j))],
        out_specs=pl.BlockSpec((tm, tn), lambda j, i: (i, j)),
        out_shape=jax.ShapeDtypeStruct((m, n), BF16),
        scratch_shapes=[pltpu.VMEM((kdim, tn), BF16), pltpu.VMEM((kdim, tn), BF16)],
        compiler_params=_cparams(("parallel", "arbitrary")),
        name="ffn_gate_up",
    )(h, wg, wu)


def _pad_lora_cols(x, axis=-1):
    o = 3 * WIDTH_B
    x = jnp.moveaxis(x, axis, -1)
    z = jnp.zeros(x.shape[:-1] + (LORA_PAD - R_DECAY,), x.dtype)
    x = jnp.concatenate([x[..., :o], x[..., o:o + R_DECAY], z,
                         x[..., o + R_DECAY:o + R_DECAY + R_AAA], z,
                         x[..., o + R_DECAY + R_AAA:]], axis=-1)
    return jnp.moveaxis(x, -1, axis)


def _unpad_lora_cols(x):
    o = 3 * WIDTH_B
    return jnp.concatenate([x[..., :o], x[..., o:o + R_DECAY], x[..., o + LORA_PAD:o + LORA_PAD + R_AAA],
                            x[..., o + 2 * LORA_PAD:]], axis=-1)


def _pad_rows(w, rows):
    return jnp.concatenate([w, jnp.zeros((rows - w.shape[0],) + w.shape[1:], w.dtype)], axis=0)


def _tiles(m):
    return dict(
        tm=min(m, 1024),
        tm_norm=min(m, 512),
        tm_prep=min(m, 256),
        tm_out=min(m, 512),
        tn=1024,
        tn_rw=896,
        tn_ffn=512,
        tk_out=D_MODEL,
        tk_ffn=1408,
        rwkv_pairs=N_HEADS_B // 2,
    )


def _group(x, pos, prev_rows, s0, n_seq, wts, tiles, q_dtype, q_scale, attention_fn):
    tm = tiles["tm"]
    h = rmsnorm_bf16(x, wts["norm_mix_pre"], tiles["tm_norm"])
    tables = rope_tables(pos)
    w_in, tn = wts["w_in"], tiles["tn"]
    (q,) = projection(h, w_in, 0, WIDTH_A, [q_dtype], tm, tn, tables, q_scale, name="proj_q")
    one_sequence = prev_rows is None
    k, *k16 = projection(h, w_in, WIDTH_A, WIDTH_A, [F32, BF16] if one_sequence else [F32], tm, tn, tables,
                         name="proj_k")
    v, *v_t = projection(h, w_in, 2 * WIDTH_A, WIDTH_A, [F32], tm, tn, values_t=one_sequence, name="proj_v")
    (rw,) = projection(h, wts["w_rw"], 0, RW_PAD_COLS, [F32], tm, tiles["tn_rw"], name="proj_rw")
    (gates,) = projection(h, wts["w_gates"], 0, 2 * D_MODEL, [F32], tm, tn, name="proj_gates")

    att = attention_fn(q, k, v, k16, v_t)

    prep = rwkv_prep(rw, prev_rows, wts["mu"], wts["w0"], wts["a0"], wts["k_k"], wts["k_a"],
                     wts["w_decay"], wts["w_aaa"], wts["w_gate"], tiles["tm_prep"])
    if one_sequence:
        rwk, s_bd = rwkv_chunked(*prep, wts["lnx_w"], wts["lnx_b"], wts["r_k"], _pair_states_to_bd(s0), n_seq,
                                 tiles["rwkv_pairs"])
    else:
        rwk, s_bd = rwkv_step(*prep, wts["lnx_w"], wts["lnx_b"], wts["r_k"], _pair_states_to_bd(s0), SUBLANES)
    s_new = _bd_to_pair_states(s_bd)

    mrg = merge_branches(att, rwk, gates, wts["w_branch_a"], wts["w_branch_b"], tm, tiles["tn"])
    x1, hf = matmul_norm_residual(mrg, wts["w_out"], x, wts["norm_mix_post"], tiles["tm_out"], tiles["tk_out"],
                                  "out_proj", g_next=wts["norm_ffn_pre"])
    act = ffn_gate_up(hf, wts["w_ffn_gate"], wts["w_ffn_up"], tm, tiles["tn_ffn"])
    y = matmul_norm_residual(act, wts["w_ffn_down"], x1, wts["norm_ffn_post"], tiles["tm_out"], tiles["tk_ffn"], "ffn_down")
    return y, k, v, rw, s_new


def kernel(x_prompt, x_sample, cache_k, cache_v, state_rwkv, state_shift, page_table, norm_mix_pre, w_in, lambda_q1, lambda_k1, lambda_q2, lambda_k2, subln_w, rw_mu, w0, w_decay, a0, w_aaa, w_gate_lora, k_k, k_a, r_k, lnx_w, lnx_b, w_branch_a, w_branch_b, w_out, norm_mix_post, norm_ffn_pre, w_ffn_gate, w_ffn_up, w_ffn_down, norm_ffn_post):
    depth = w_in.shape[0]
    bsz, seq = x_prompt.shape[0], x_prompt.shape[1]
    db, ds = x_sample.shape[0], x_sample.shape[1]
    assert depth == 1 and bsz == 1 and ds == 1
    n_pool = cache_k.shape[1]
    n_pages = page_table.shape[1]
    past = n_pages * PAGE_SIZE
    l = 0
    lambda_init = 0.8 - 0.6 * math.exp(-0.3 * l)

    wi = w_in[l].T
    row = lambda p: p.reshape(1, -1)
    wts = {
        "norm_mix_pre": norm_mix_pre[l],
        "w_in": wi,
        "w_rw": _pad_lora_cols(wi[A_COLS:A_COLS + RW_COLS], axis=0),
        "w_gates": wi[A_COLS + RW_COLS:],
        "mu": _pad_lora_cols(row(rw_mu[l])),
        "w0": row(w0[l]), "a0": row(a0[l]), "k_k": row(k_k[l]), "k_a": row(k_a[l]),
        "w_decay": _pad_rows(w_decay[l], LORA_PAD).astype(BF16),
        "w_aaa": _pad_rows(w_aaa[l], LORA_PAD).astype(BF16),
        "w_gate": w_gate_lora[l].astype(BF16),
        "lnx_w": lnx_w[l], "lnx_b": lnx_b[l], "r_k": r_k[l],
        "w_branch_a": w_branch_a[l].astype(BF16), "w_branch_b": w_branch_b[l].astype(BF16),
        "w_out": w_out[l].astype(BF16), "norm_mix_post": norm_mix_post[l],
        "norm_ffn_pre": norm_ffn_pre[l],
        "w_ffn_gate": w_ffn_gate[l], "w_ffn_up": w_ffn_up[l],
        "w_ffn_down": w_ffn_down[l].astype(BF16), "norm_ffn_post": norm_ffn_post[l],
    }
    lam_params = [row(p[l]) for p in (lambda_q1, lambda_k1, lambda_q2, lambda_k2)]
    subln = subln_w[l]

    tiles_p = _tiles(seq)

    def prompt_attention(q, k, v, k16, v_t):
        del k, v
        return prompt_diff_attention(q, k16[0], v_t[0], lam_params, subln, lambda_init, 512, 512)

    s0_p = jnp.zeros((bsz, N_HEADS_B, D_HEAD_B, D_HEAD_B), F32)
    y_p, k_p, v_p, rw_p, s_p = _group(x_prompt.reshape(seq, D_MODEL), jnp.arange(seq), None, s0_p, 1,
                                      wts, tiles_p, BF16, LOG2_E * D_HEAD_A ** -0.5, prompt_attention)

    tiles_s = _tiles(db)
    ck = jnp.transpose(cache_k[l], (0, 2, 3, 1)).reshape(n_pool, WIDTH_A, PAGE_SIZE)
    cv = cache_v[l]

    def sample_attention(q, k, v, k16, v_t):
        del k16, v_t
        return sample_diff_attention(q, k, v, ck, cv, page_table, lam_params, subln, lambda_init, 8)

    y_s, k_s, v_s, rw_s, s_s = _group(x_sample.reshape(db, D_MODEL), jnp.full((db,), past, jnp.int32),
                                      _pad_lora_cols(state_shift[l]),
                                      state_rwkv[l], db, wts, tiles_s, F32, D_HEAD_A ** -0.5, sample_attention)

    return (y_p.reshape(bsz, seq, D_MODEL),
            y_s.reshape(db, ds, D_MODEL),
            k_p.reshape(1, bsz, seq, 2 * N_HEADS_A, D_HEAD_A),
            v_p.reshape(1, bsz, seq, N_HEADS_A, 2 * D_HEAD_A),
            s_p.reshape(1, bsz, N_HEADS_B, D_HEAD_B, D_HEAD_B),
            _unpad_lora_cols(rw_p[-1:]).reshape(1, bsz, RW_COLS),
            k_s.reshape(1, db, ds, 2 * N_HEADS_A, D_HEAD_A),
            v_s.reshape(1, db, ds, N_HEADS_A, 2 * D_HEAD_A),
            s_s.reshape(1, db, N_HEADS_B, D_HEAD_B, D_HEAD_B),
            _unpad_lora_cols(rw_s).reshape(1, db, RW_COLS))
```

```python
import functools
import math

import jax
import jax.numpy as jnp
from jax import lax
from jax.experimental import pallas as pl
from jax.experimental.pallas import tpu as pltpu

F32 = jnp.float32
BF16 = jnp.bfloat16

D_MODEL = 2048
N_HEADS_A = 8
D_HEAD_A = 64
WIDTH_A = 2 * N_HEADS_A * D_HEAD_A
N_HEADS_B = 16
D_HEAD_B = 64
WIDTH_B = N_HEADS_B * D_HEAD_B
R_DECAY = 96
R_AAA = 96
R_GATE = 256
ROPE_THETA = 10000.0
NORM_EPS = 1e-6
SUBLN_EPS = 1e-5
GN_EPS = 64e-5
LOG2_E = math.log2(math.e)
PAGE_SIZE = 128
A_COLS = 3 * WIDTH_A
RW_COLS = 3 * WIDTH_B + R_DECAY + R_AAA + R_GATE

LANES = 128
SUBLANES = 8
ONES_ROWS = 16
KV_BLOCKS_IN_FLIGHT = 2
LORA_PAD = 128
RW_PAD_COLS = 3 * WIDTH_B + 2 * LORA_PAD + R_GATE
CHUNK = 64
INV_BLOCK = 16
VMEM_LIMIT = 56 * 1024 * 1024


def _cparams(sem):
    return pltpu.CompilerParams(dimension_semantics=sem, vmem_limit_bytes=VMEM_LIMIT)


def _dot(a, b):
    return jnp.dot(a, b, preferred_element_type=F32)


def _dot_nt(a, b):
    return lax.dot_general(a, b, (((1,), (1,)), ((), ())), preferred_element_type=F32)


def _dot_tn(a, b):
    return lax.dot_general(a, b, (((0,), (0,)), ((), ())), preferred_element_type=F32)


def _sigmoid(x):
    return 0.5 * jnp.tanh(0.5 * x) + 0.5


def _rms(x, g, eps):
    return x * lax.rsqrt(jnp.mean(x * x, axis=-1, keepdims=True) + eps) * g


def _norm_kernel(x_ref, g_ref, o_ref):
    o_ref[...] = _rms(x_ref[...], g_ref[...], NORM_EPS).astype(o_ref.dtype)


def rmsnorm_bf16(x, g, tm):
    m, d = x.shape
    return pl.pallas_call(
        _norm_kernel,
        grid=(m // tm,),
        in_specs=[pl.BlockSpec((tm, d), lambda i: (i, 0)),
                  pl.BlockSpec((1, d), lambda i: (0, 0))],
        out_specs=pl.BlockSpec((tm, d), lambda i: (i, 0)),
        out_shape=jax.ShapeDtypeStruct((m, d), BF16),
        compiler_params=_cparams(("parallel",)),
        name="rmsnorm_bf16",
    )(x, g.reshape(1, d))


def _rope_tile(x, cos, sin_signed):
    lane = lax.broadcasted_iota(jnp.int32, x.shape, 1)
    first_half = (lane % D_HEAD_A) < (D_HEAD_A // 2)
    partner = jnp.where(first_half,
                        pltpu.roll(x, LANES - D_HEAD_A // 2, 1),
                        pltpu.roll(x, D_HEAD_A // 2, 1))
    return x * cos + partner * sin_signed


def _proj_kernel(*refs, rope, scale, n_out, values_t):
    if rope:
        a_ref, w_ref, cos_ref, sin_ref = refs[:4]
        outs = refs[4:-1]
    else:
        a_ref, w_ref = refs[:2]
        outs = refs[2:-1]
    w16_ref = refs[-1]

    @pl.when(pl.program_id(1) == 0)
    def _():
        w16_ref[...] = w_ref[...].astype(BF16)

    acc = _dot_nt(a_ref[...], w16_ref[...])
    if rope:
        cos = cos_ref[...]
        sin = sin_ref[...]
        tn = acc.shape[1]
        acc = jnp.concatenate(
            [_rope_tile(acc[:, c * LANES:(c + 1) * LANES], cos, sin) for c in range(tn // LANES)], axis=1)
    if scale != 1.0:
        acc = acc * scale
    for o_ref in outs[:n_out]:
        o_ref[...] = acc.astype(o_ref.dtype)
    if values_t:
        t_ref = outs[n_out]
        acc_t = jnp.transpose(acc)
        for grp in range(acc.shape[1] // LANES):
            t_ref[grp, 0:LANES, :] = acc_t[grp * LANES:(grp + 1) * LANES, :].astype(t_ref.dtype)
            t_ref[grp, LANES:, :] = jnp.ones((ONES_ROWS, acc.shape[0]), t_ref.dtype)


def projection(a, w, col0, n, out_dtypes, tm, tn, rope_tables=None, scale=1.0, values_t=False,
               name="projection"):
    m, kdim = a.shape
    j0 = col0 // tn
    in_specs = [pl.BlockSpec((tm, kdim), lambda j, i: (i, 0)),
                pl.BlockSpec((tn, kdim), lambda j, i: (j + j0, 0))]
    args = [a, w]
    if rope_tables is not None:
        in_specs += [pl.BlockSpec((tm, LANES), lambda j, i: (i, 0))] * 2
        args += list(rope_tables)
    out_specs = [pl.BlockSpec((tm, tn), lambda j, i: (i, j)) for _ in out_dtypes]
    out_shape = [jax.ShapeDtypeStruct((m, n), dt) for dt in out_dtypes]
    if values_t:
        rows = LANES + ONES_ROWS
        out_specs.append(pl.BlockSpec((tn // LANES, rows, tm), lambda j, i: (j, 0, i)))
        out_shape.append(jax.ShapeDtypeStruct((n // LANES, rows, m), BF16))
    outs = pl.pallas_call(
        functools.partial(_proj_kernel, rope=rope_tables is not None, scale=scale, n_out=len(out_dtypes),
                          values_t=values_t),
        grid=(n // tn, m // tm),
        in_specs=in_specs,
        out_specs=out_specs,
        out_shape=out_shape,
        scratch_shapes=[pltpu.VMEM((tn, kdim), BF16)],
        compiler_params=_cparams(("parallel", "arbitrary")),
        name=name,
    )(*args)
    return outs


def rope_tables(pos):
    half = D_HEAD_A // 2
    inv = 1.0 / (ROPE_THETA ** (jnp.arange(half, dtype=F32) / half))
    ang = pos.astype(F32)[:, None] * inv[None, :]
    cos = jnp.cos(ang)
    sin = jnp.sin(ang)
    cos2 = jnp.concatenate([cos, cos, cos, cos], axis=1)
    sin2 = jnp.concatenate([-sin, sin, -sin, sin], axis=1)
    return cos2, sin2


def _diff_lambda(lq1, lk1, lq2, lk2, lambda_init):
    s1 = jnp.sum(lq1 * lk1, axis=-1, keepdims=True)
    s2 = jnp.sum(lq2 * lk2, axis=-1, keepdims=True)
    return jnp.exp(s1) - jnp.exp(s2) + lambda_init


def _flash_kernel(q_ref, k_ref, v_ref, lq1_ref, lk1_ref, lq2_ref, lk2_ref, g_ref, o_ref,
                  qs_ref, m_ref, acc_ref, s_ref, p_ref, *, tq, tk, col_chunk, row_chunk, lambda_init):
    qi = pl.program_id(1)
    ratio = tq // tk
    qt = jnp.transpose(q_ref[...].astype(F32))
    qt2 = jnp.concatenate([qt, qt], axis=1)
    row = lax.broadcasted_iota(jnp.int32, qt2.shape, 0)
    col = lax.broadcasted_iota(jnp.int32, qt2.shape, 1)
    qs_ref[...] = jnp.where((row < D_HEAD_A) == (col < tq), qt2, 0.0).astype(BF16)
    m_ref[...] = jnp.full(m_ref.shape, -jnp.inf, F32)
    acc_ref[...] = jnp.zeros(acc_ref.shape, F32)

    def column_chunk(c, slot, k_blk, v_t, kv, masked):
        cols = slice(c * col_chunk, (c + 1) * col_chunk)
        rows = tk
        if masked:
            rows = max(0, min(tk, (c * col_chunk) % tq + col_chunk - slot * tk))
            if rows == 0:
                return
        s_ref[slot, 0:rows, cols] = _dot(k_blk[0:rows], qs_ref[:, cols])
        yield
        sub = SUBLANES
        groups = row_chunk // sub

        def scores(r0):
            blk = s_ref[slot, r0:r0 + row_chunk, cols]
            if masked:
                krow = lax.broadcasted_iota(jnp.int32, blk.shape, 0) + r0
                qcol = lax.broadcasted_iota(jnp.int32, blk.shape, 1) + (c * col_chunk) % tq
                blk = jnp.where(kv * tk + krow <= qi * tq + qcol, blk, -jnp.inf)
            return blk

        part = None
        for r0 in range(0, rows, row_chunk):
            mx = jnp.max(scores(r0).reshape(groups, sub, col_chunk), axis=0)
            part = mx if part is None else jnp.maximum(part, mx)
        m_prev = m_ref[:, cols]
        m_new = jnp.maximum(m_prev, jnp.max(part, axis=0, keepdims=True))
        alpha = jnp.exp2(m_prev - m_new)
        for r0 in range(0, rows, row_chunk):
            p_ref[slot, r0:r0 + row_chunk, cols] = jnp.exp2(scores(r0) - m_new).astype(BF16)
        m_ref[:, cols] = m_new
        pv = _dot(v_t[:, 0:rows], p_ref[slot, 0:rows, cols])
        yield
        acc_ref[:, cols] = alpha * acc_ref[:, cols] + pv

    def blocks(kvs, masked):
        chunks = []
        for slot, kv in enumerate(kvs):
            start = pl.multiple_of(kv * tk, tk)
            k_blk = k_ref[pl.ds(start, tk), :]
            v_t = v_ref[0, :, pl.ds(start, tk)]
            chunks += [column_chunk(c, slot, k_blk, v_t, kv, masked) for c in range(2 * tq // col_chunk)]
        _run_interleaved(chunks)

    n_full = qi * ratio
    slots = s_ref.shape[0]

    def full_blocks(j, carry):
        blocks([j * slots + s for s in range(slots)], False)
        return carry

    lax.fori_loop(0, n_full // slots, full_blocks, 0)
    for rem in range(1, slots):
        @pl.when(n_full % slots == rem)
        def _(rem=rem):
            blocks([n_full - rem + s for s in range(rem)], False)
    blocks([n_full + r for r in range(ratio)], True)

    lam = _diff_lambda(lq1_ref[...], lk1_ref[...], lq2_ref[...], lk2_ref[...], lambda_init)
    o = acc_ref[0:LANES, :] * (1.0 / acc_ref[LANES:LANES + 1, :])
    o = jnp.transpose(o[:, 0:tq] - lam * o[:, tq:2 * tq])
    o = _rms(o, g_ref[...], SUBLN_EPS) * (1.0 - lambda_init)
    o_ref[...] = o.astype(o_ref.dtype)


def prompt_diff_attention(q, k, v_t, lam_params, subln_w, lambda_init, tq, tk):
    t = q.shape[0]
    n_pairs = N_HEADS_A
    assert tq % tk == 0 and tq // tk <= KV_BLOCKS_IN_FLIGHT
    small = pl.BlockSpec((1, D_HEAD_A), lambda h, i: (0, 0))
    whole = pl.BlockSpec((t, LANES), lambda h, i: (0, h))
    whole_t = pl.BlockSpec((1, LANES + ONES_ROWS, t), lambda h, i: (h, 0, 0))
    return pl.pallas_call(
        functools.partial(_flash_kernel, tq=tq, tk=tk, col_chunk=2 * LANES, row_chunk=2 * SUBLANES,
                          lambda_init=lambda_init),
        grid=(n_pairs, t // tq),
        in_specs=[pl.BlockSpec((tq, LANES), lambda h, i: (i, h)),
                  whole, whole_t,
                  small, small, small, small,
                  pl.BlockSpec((1, LANES), lambda h, i: (0, 0))],
        out_specs=pl.BlockSpec((tq, LANES), lambda h, i: (i, h)),
        out_shape=jax.ShapeDtypeStruct((t, WIDTH_A), BF16),
        scratch_shapes=[pltpu.VMEM((LANES, 2 * tq), BF16),
                        pltpu.VMEM((1, 2 * tq), F32),
                        pltpu.VMEM((LANES + ONES_ROWS, 2 * tq), F32),
                        pltpu.VMEM((KV_BLOCKS_IN_FLIGHT, tk, 2 * tq), F32),
                        pltpu.VMEM((KV_BLOCKS_IN_FLIGHT, tk, 2 * tq), BF16)],
        compiler_params=_cparams(("parallel", "arbitrary")),
        name="prompt_diff_attention",
    )(q, k, v_t, *lam_params, subln_w.reshape(1, LANES))


def _decode_kernel(pt_ref, q_ref, kn_ref, vn_ref, lq1_ref, lk1_ref, lq2_ref, lk2_ref, g_ref, *rest,
                   pages_per_step, lambda_init):
    del pt_ref
    k_refs = rest[:pages_per_step]
    v_refs = rest[pages_per_step:2 * pages_per_step]
    o_ref = rest[2 * pages_per_step]
    qm_ref, ex_ref, m_ref, l_ref, acc_ref = rest[2 * pages_per_step + 1:]
    hp = N_HEADS_A
    n_rows = 2 * hp
    j = pl.program_id(1)

    def pair_mask():
        row = lax.broadcasted_iota(jnp.int32, (n_rows, PAGE_SIZE * hp), 0)
        lane = lax.broadcasted_iota(jnp.int32, (n_rows, PAGE_SIZE * hp), 1)
        return (lane % hp) == (row % hp)

    @pl.when(j == 0)
    def _():
        row = lax.broadcasted_iota(jnp.int32, (n_rows, WIDTH_A), 0)
        lane = lax.broadcasted_iota(jnp.int32, (n_rows, WIDTH_A), 1)
        qb = jnp.broadcast_to(q_ref[0], (n_rows, WIDTH_A))
        qm_ref[...] = jnp.where(lane // D_HEAD_A == 2 * (row % hp) + row // hp, qb, 0.0)
        tok = lax.broadcasted_iota(jnp.int32, ex_ref.shape, 0)
        pos = lax.broadcasted_iota(jnp.int32, ex_ref.shape, 1)
        ex_ref[...] = jnp.where(pos // hp == tok, 1.0, 0.0).astype(BF16)
        m_ref[...] = jnp.full(m_ref.shape, -jnp.inf, F32)
        l_ref[...] = jnp.zeros(l_ref.shape, F32)
        acc_ref[...] = jnp.zeros(acc_ref.shape, F32)

    qm = qm_ref[...].astype(BF16)
    keep = pair_mask()
    pages = range(pages_per_step)
    s = jnp.concatenate([_dot(qm, k_refs[i][0].astype(BF16)) for i in pages], axis=1)
    m_prev = m_ref[...]
    m_new = jnp.maximum(m_prev, jnp.max(s, axis=-1, keepdims=True))
    alpha = jnp.exp(m_prev - m_new)
    p = jnp.exp(s - m_new)
    l_ref[...] = alpha * l_ref[...] + jnp.sum(p, axis=-1, keepdims=True)
    p16 = p.astype(BF16)
    p_wide = [jnp.where(keep, _dot(p16[:, i * PAGE_SIZE:(i + 1) * PAGE_SIZE], ex_ref[...]), 0.0).astype(BF16)
              for i in pages]
    pv = _dot(p_wide[0], v_refs[0][0].reshape(PAGE_SIZE * hp, LANES).astype(BF16))
    for i in pages[1:]:
        pv += _dot(p_wide[i], v_refs[i][0].reshape(PAGE_SIZE * hp, LANES).astype(BF16))
    acc_ref[...] = alpha * acc_ref[...] + pv
    m_ref[...] = m_new

    @pl.when(j == pl.num_programs(1) - 1)
    def _():
        s = jnp.sum(qm_ref[...] * kn_ref[0], axis=-1, keepdims=True)
        m_prev = m_ref[...]
        m_new = jnp.maximum(m_prev, s)
        alpha = jnp.exp(m_prev - m_new)
        p = jnp.exp(s - m_new)
        l = alpha * l_ref[...] + p
        vn = vn_ref[0]
        acc = alpha * acc_ref[...] + p * jnp.concatenate([vn, vn], axis=0)
        lam = _diff_lambda(lq1_ref[...], lk1_ref[...], lq2_ref[...], lk2_ref[...], lambda_init)
        o = acc / l
        o = o[0:hp, :] - lam * o[hp:n_rows, :]
        o_ref[0] = (_rms(o, g_ref[...], SUBLN_EPS) * (1.0 - lambda_init)).astype(o_ref.dtype)


def sample_diff_attention(q, k_new, v_new, cache_k, cache_v, page_table, lam_params, subln_w,
                          lambda_init, pages_per_step):
    b = q.shape[0]
    n_pages = page_table.shape[1]
    steps = n_pages // pages_per_step
    hp = N_HEADS_A
    row3 = lambda x: x.reshape(b, 1, WIDTH_A)
    vec = pl.BlockSpec((1, 1, WIDTH_A), lambda bi, j, pt: (bi, 0, 0))
    small = pl.BlockSpec((1, D_HEAD_A), lambda bi, j, pt: (0, 0))
    pairs = pl.BlockSpec((1, hp, LANES), lambda bi, j, pt: (bi, 0, 0))

    def page_spec(i, shape):
        zeros = (0,) * len(shape)
        return pl.BlockSpec((1,) + shape,
                            lambda bi, j, pt: (pt[bi * n_pages + j * pages_per_step + i],) + zeros)

    grid_spec = pltpu.PrefetchScalarGridSpec(
        num_scalar_prefetch=1,
        grid=(b, steps),
        in_specs=[vec, vec, pairs, small, small, small, small,
                  pl.BlockSpec((1, LANES), lambda bi, j, pt: (0, 0))]
                 + [page_spec(i, (WIDTH_A, PAGE_SIZE)) for i in range(pages_per_step)]
                 + [page_spec(i, (PAGE_SIZE, hp, LANES)) for i in range(pages_per_step)],
        out_specs=pairs,
        scratch_shapes=[pltpu.VMEM((2 * hp, WIDTH_A), F32),
                        pltpu.VMEM((PAGE_SIZE, PAGE_SIZE * hp), BF16),
                        pltpu.VMEM((2 * hp, 1), F32),
                        pltpu.VMEM((2 * hp, 1), F32),
                        pltpu.VMEM((2 * hp, LANES), F32)],
    )
    out = pl.pallas_call(
        functools.partial(_decode_kernel, pages_per_step=pages_per_step, lambda_init=lambda_init),
        grid_spec=grid_spec,
        out_shape=jax.ShapeDtypeStruct((b, hp, LANES), BF16),
        compiler_params=_cparams(("parallel", "arbitrary")),
        name="sample_diff_attention",
    )(page_table.reshape(-1), row3(q), row3(k_new), v_new.reshape(b, hp, LANES), *lam_params,
      subln_w.reshape(1, LANES), *([cache_k] * pages_per_step), *([cache_v] * pages_per_step))
    return out.reshape(b, WIDTH_A)


def _rwkv_prep_kernel(c_ref, p_ref, mu_ref, w0_ref, a0_ref, kk_ref, ka_ref, wdec_ref, waaa_ref, wgate_ref,
                      r_ref, lw_ref, k_ref, v_ref, kkr_ref, a_ref, g_ref, *, one_sequence):
    c = c_ref[...]
    if one_sequence:
        above = jnp.where(pl.program_id(0) == 0, 0.0, p_ref[SUBLANES - 1:SUBLANES, :])
        row = lax.broadcasted_iota(jnp.int32, c.shape, 0)
        prev = jnp.where(row == 0, above, pltpu.roll(c, 1, 0))
    else:
        prev = p_ref[...]
    u = c + mu_ref[...] * (prev - c)
    wb = WIDTH_B
    r = u[:, 0:wb]
    kx = u[:, wb:2 * wb]
    v = u[:, 2 * wb:3 * wb]
    o = 3 * wb
    wd = u[:, o:o + LORA_PAD]
    ad = u[:, o + LORA_PAD:o + 2 * LORA_PAD]
    gd = u[:, o + 2 * LORA_PAD:]
    z = w0_ref[...] + _dot(jnp.tanh(wd).astype(BF16), wdec_ref[...])
    nz = -z
    softplus = jnp.maximum(nz, 0.0) + jnp.log(1.0 + jnp.exp(-jnp.abs(nz)))
    w_raw = -softplus - 0.5
    a = _sigmoid(a0_ref[...] + _dot(ad.astype(BF16), waaa_ref[...]))
    r_ref[...] = r
    lw_ref[...] = -jnp.exp(w_raw)
    k_ref[...] = kx * (1.0 + (a - 1.0) * ka_ref[...])
    v_ref[...] = v
    kkr_ref[...] = kx * kk_ref[...]
    a_ref[...] = a
    g_ref[...] = _dot(_sigmoid(gd).astype(BF16), wgate_ref[...])


def rwkv_prep(cols, prev_rows, mu, w0, a0, k_k, k_a, w_decay, w_aaa, w_gate, tm):
    m = cols.shape[0]
    wb = WIDTH_B
    row = lambda n: pl.BlockSpec((1, n), lambda i: (0, 0))
    full = lambda r_, c_: pl.BlockSpec((r_, c_), lambda i: (0, 0))
    tile = lambda n: pl.BlockSpec((tm, n), lambda i: (i, 0))
    one_sequence = prev_rows is None
    if one_sequence:
        prev_rows = cols
        prev_spec = pl.BlockSpec((SUBLANES, RW_PAD_COLS),
                                 lambda i: (jnp.maximum(i * (tm // SUBLANES) - 1, 0), 0))
    else:
        prev_spec = tile(RW_PAD_COLS)
    return pl.pallas_call(
        functools.partial(_rwkv_prep_kernel, one_sequence=one_sequence),
        grid=(m // tm,),
        in_specs=[tile(RW_PAD_COLS), prev_spec, row(RW_PAD_COLS), row(wb), row(wb), row(wb), row(wb),
                  full(LORA_PAD, wb), full(LORA_PAD, wb), full(R_GATE, wb)],
        out_specs=[tile(wb)] * 7,
        out_shape=[jax.ShapeDtypeStruct((m, wb), F32)] * 7,
        compiler_params=_cparams(("parallel",)),
        name="rwkv_prep",
    )(cols, prev_rows, mu, w0, a0, k_k, k_a, w_decay, w_aaa, w_gate)


def _rwkv_pair_chunk(r, lw, k, v, kkr, a, g, lnw, lnb, rk, s_prev):
    cl = CHUNK
    n = D_HEAD_B

    def stack(x):
        x2 = jnp.concatenate([x, x], axis=0)
        row = lax.broadcasted_iota(jnp.int32, x2.shape, 0)
        lane = lax.broadcasted_iota(jnp.int32, x2.shape, 1)
        return jnp.where((row // cl) == (lane // n), x2, 0.0)

    def fold(x_bd):
        return x_bd[0:cl, :] + x_bd[cl:2 * cl, :]

    row_c = lax.broadcasted_iota(jnp.int32, lw.shape, 0)
    cum = lw
    shift = 1
    while shift < cl:
        cum = cum + jnp.where(row_c >= shift, pltpu.roll(cum, shift, 0), 0.0)
        shift *= 2
    cum_last = cum[cl - 1:cl, :]

    kk_bd = stack(kkr)
    norm = jnp.sqrt(jnp.sum(kk_bd * kk_bd, axis=-1, keepdims=True))
    kk = fold(kk_bd * (1.0 / jnp.maximum(norm, 1e-12)))
    b = kk * a

    e_pos = jnp.exp(cum)
    e_neg = jnp.exp(-cum)
    e_rest = jnp.exp(cum_last - cum)
    r_t = stack(r * e_pos)
    a_t = stack(-kk * jnp.exp(cum - lw))
    b_t = b * e_neg
    k_t = k * e_neg
    b_h = stack(b * e_rest)
    k_h = stack(k * e_rest)
    v_bd = stack(v)
    gamma = jnp.exp(cum_last)

    lhs = jnp.concatenate([a_t, r_t], axis=0).astype(BF16)
    rhs = jnp.concatenate([b_t, b_t, k_t, k_t], axis=0).astype(BF16)
    m1 = _dot_nt(lhs, rhs)
    xr = _dot_nt(lhs, s_prev.astype(BF16))
    yield
    c2 = 2 * cl
    row = lax.broadcasted_iota(jnp.int32, (c2, c2), 0)
    col = lax.broadcasted_iota(jnp.int32, (c2, c2), 1)
    same_head = (row // cl) == (col // cl)
    strict = jnp.logical_and(same_head, row > col)
    incl = jnp.logical_and(same_head, row >= col)
    a_ab = jnp.where(strict, m1[0:c2, 0:c2], 0.0)
    a_ak = jnp.where(strict, m1[0:c2, c2:2 * c2], 0.0)
    a_rb = jnp.where(incl, m1[c2:2 * c2, 0:c2], 0.0)
    a_rk = jnp.where(incl, m1[c2:2 * c2, c2:2 * c2], 0.0)

    mm = lambda x, y: _dot(x.astype(BF16), y.astype(BF16))
    eye = jnp.where(row == col, 1.0, 0.0)
    diag_blk = (row // INV_BLOCK) == (col // INV_BLOCK)
    x = jnp.where(diag_blk, a_ab, 0.0)
    a_off = a_ab - x
    t_d = eye + x
    vv = mm(jnp.concatenate([a_ak, a_rk], axis=0), v_bd)
    span = 2
    while span < INV_BLOCK:
        x = mm(x, x)
        yield
        t_d = t_d + mm(t_d, x)
        span *= 2
    yield
    nm = mm(t_d, a_off)
    yield
    nm_pow = nm
    m_acc = eye + nm
    blocks = 2
    while blocks < cl // INV_BLOCK:
        sq = mm(nm_pow, nm_pow)
        yield
        m_acc = m_acc + mm(m_acc, sq)
        yield
        nm_pow = sq
        blocks *= 2
    t_inv = mm(m_acc, t_d)
    yield
    u = mm(t_inv, xr[0:c2, :] + vv[0:c2, :])
    yield
    y_bd = xr[c2:2 * c2, :] + mm(a_rb, u) + vv[c2:2 * c2, :]
    s_new = s_prev * gamma + _dot_tn(jnp.concatenate([u, v_bd], axis=0).astype(BF16),
                                     jnp.concatenate([b_h, k_h], axis=0).astype(BF16))
    yield

    lane2 = lax.broadcasted_iota(jnp.int32, (c2, LANES), 1)
    row2 = lax.broadcasted_iota(jnp.int32, (c2, LANES), 0)
    head_mask = (row2 // cl) == (lane2 // n)
    mean = jnp.sum(y_bd, axis=-1, keepdims=True) * (1.0 / n)
    d = jnp.where(head_mask, y_bd - mean, 0.0)
    var = jnp.sum(d * d, axis=-1, keepdims=True) * (1.0 / n)
    yn = fold(d * lax.rsqrt(var + GN_EPS))
    yn = yn * lnw + lnb
    rk_sum = jnp.sum(stack(r * k * rk), axis=-1, keepdims=True)
    bonus = fold(rk_sum * v_bd)
    return (yn + bonus) * g, s_new


def _run_interleaved(generators):
    results = [None] * len(generators)
    active = list(enumerate(generators))
    while active:
        still = []
        for i, gen in active:
            try:
                next(gen)
                still.append((i, gen))
            except StopIteration as done:
                results[i] = done.value
        active = still
    return results


def _rwkv_chunk_kernel(r_ref, lw_ref, k_ref, v_ref, kkr_ref, a_ref, g_ref, lnw_ref, lnb_ref, rk_ref, s0_ref,
                       y_ref, s_out_ref, s_ref, *, pairs_per_step):
    c = pl.program_id(2)

    @pl.when(c == 0)
    def _():
        s_ref[...] = s0_ref[0]

    in_refs = (r_ref, lw_ref, k_ref, v_ref, kkr_ref, a_ref, g_ref, lnw_ref, lnb_ref, rk_ref)
    lanes = [slice(p * LANES, (p + 1) * LANES) for p in range(pairs_per_step)]
    loaded = [tuple(ref[:, ln] for ref in in_refs) + (s_ref[p],) for p, ln in enumerate(lanes)]
    results = _run_interleaved([_rwkv_pair_chunk(*args) for args in loaded])
    for p, (y, s_new) in enumerate(results):
        y_ref[:, lanes[p]] = y.astype(y_ref.dtype)
        s_ref[p] = s_new

    @pl.when(c == pl.num_programs(2) - 1)
    def _():
        s_out_ref[0] = s_ref[...]


def rwkv_chunked(r, lw, k, v, kkr, a, g, lnx_w, lnx_b, r_k, s0_bd, n_seq, pairs_per_step):
    m = r.shape[0]
    n_pairs = N_HEADS_B // 2
    nc = m // n_seq // CHUNK
    pps = pairs_per_step
    tile = pl.BlockSpec((CHUNK, pps * LANES), lambda s, h, c: (s * nc + c, h))
    par = pl.BlockSpec((1, pps * LANES), lambda s, h, c: (0, h))
    st = pl.BlockSpec((1, pps, LANES, LANES), lambda s, h, c: (s, h, 0, 0))
    return pl.pallas_call(
        functools.partial(_rwkv_chunk_kernel, pairs_per_step=pps),
        grid=(n_seq, n_pairs // pps, nc),
        in_specs=[tile] * 7 + [par] * 3 + [st],
        out_specs=[tile, st],
        out_shape=[jax.ShapeDtypeStruct((m, WIDTH_B), BF16),
                   jax.ShapeDtypeStruct((n_seq, n_pairs, LANES, LANES), F32)],
        scratch_shapes=[pltpu.VMEM((pps, LANES, LANES), F32)],
        compiler_params=_cparams(("parallel", "parallel", "arbitrary")),
        name="rwkv_chunked",
    )(r, lw, k, v, kkr, a, g, lnx_w.reshape(1, -1), lnx_b.reshape(1, -1), r_k.reshape(1, -1), s0_bd)


def _rwkv_step_kernel(r_ref, lw_ref, k_ref, v_ref, kkr_ref, a_ref, g_ref, lnw_ref, lnb_ref, rk_ref, s_ref,
                      y_ref, s_out_ref):
    n = D_HEAD_B
    seqs = r_ref.shape[0]
    lane = lax.broadcasted_iota(jnp.int32, (seqs, LANES), 1)
    first = lane < n
    sq_row = lax.broadcasted_iota(jnp.int32, (LANES, LANES), 0)
    sq_lane = lax.broadcasted_iota(jnp.int32, (LANES, LANES), 1)
    same_head = (sq_row // n) == (sq_lane // n)

    def head_sums(x):
        lo = jnp.sum(jnp.where(first, x, 0.0), axis=-1, keepdims=True)
        hi = jnp.sum(jnp.where(first, 0.0, x), axis=-1, keepdims=True)
        return jnp.where(first, lo, hi)

    def to_columns(x):
        return jnp.transpose(jnp.concatenate([x, jnp.zeros((LANES - seqs, LANES), F32)], axis=0))

    for p in range(s_ref.shape[1]):
        ln = slice(p * LANES, (p + 1) * LANES)
        r, k, v, a = r_ref[:, ln], k_ref[:, ln], v_ref[:, ln], a_ref[:, ln]
        kkr = kkr_ref[:, ln]
        kk = kkr * (1.0 / jnp.maximum(jnp.sqrt(head_sums(kkr * kkr)), 1e-12))
        b = kk * a
        decay = jnp.exp(lw_ref[:, ln])
        v_cols = to_columns(v)
        y_cols = jnp.zeros((LANES, LANES), F32)
        for s in range(seqs):
            row = slice(s, s + 1)
            state = s_ref[s, p]
            sa = jnp.sum(state * (-kk[row]), axis=-1, keepdims=True)
            update = sa * b[row] + v_cols[:, s:s + 1] * k[row]
            state = state * decay[row] + jnp.where(same_head, update, 0.0)
            s_out_ref[s, p] = state
            y_col = jnp.sum(state * r[row], axis=-1, keepdims=True)
            y_cols = jnp.where(sq_lane == s, y_col, y_cols)
        y = jnp.transpose(y_cols)[0:seqs, :]
        mean = head_sums(y) * (1.0 / n)
        d = y - mean
        var = head_sums(d * d) * (1.0 / n)
        yn = d * lax.rsqrt(var + GN_EPS) * lnw_ref[:, ln] + lnb_ref[:, ln]
        bonus = head_sums(r * k * rk_ref[:, ln]) * v
        y_ref[:, ln] = ((yn + bonus) * g_ref[:, ln]).astype(y_ref.dtype)


def rwkv_step(r, lw, k, v, kkr, a, g, lnx_w, lnx_b, r_k, s0_bd, seqs_per_step):
    n_seq = r.shape[0]
    n_pairs = N_HEADS_B // 2
    tile = pl.BlockSpec((seqs_per_step, WIDTH_B), lambda i: (i, 0))
    par = pl.BlockSpec((1, WIDTH_B), lambda i: (0, 0))
    st = pl.BlockSpec((seqs_per_step, n_pairs, LANES, LANES), lambda i: (i, 0, 0, 0))
    return pl.pallas_call(
        _rwkv_step_kernel,
        grid=(n_seq // seqs_per_step,),
        in_specs=[tile] * 7 + [par] * 3 + [st],
        out_specs=[tile, st],
        out_shape=[jax.ShapeDtypeStruct((n_seq, WIDTH_B), BF16),
                   jax.ShapeDtypeStruct((n_seq, n_pairs, LANES, LANES), F32)],
        compiler_params=_cparams(("parallel",)),
        name="rwkv_step",
    )(r, lw, k, v, kkr, a, g, lnx_w.reshape(1, -1), lnx_b.reshape(1, -1), r_k.reshape(1, -1), s0_bd)


def _pair_states_to_bd(s):
    bsz = s.shape[0]
    s = s.reshape(bsz, N_HEADS_B // 2, 2, D_HEAD_B, D_HEAD_B)
    z = jnp.zeros_like(s[:, :, 0])
    top = jnp.concatenate([s[:, :, 0], z], axis=-1)
    bot = jnp.concatenate([z, s[:, :, 1]], axis=-1)
    return jnp.concatenate([top, bot], axis=-2)


def _bd_to_pair_states(s_bd):
    bsz = s_bd.shape[0]
    n = D_HEAD_B
    s0 = s_bd[:, :, 0:n, 0:n]
    s1 = s_bd[:, :, n:2 * n, n:2 * n]
    return jnp.stack([s0, s1], axis=2).reshape(bsz, N_HEADS_B, n, n)


def _merge_kernel(att_ref, rwk_ref, ga_ref, gb_ref, wa_ref, wb_ref, o_ref):
    ya = _dot(att_ref[...], wa_ref[...])
    yb = _dot(rwk_ref[...], wb_ref[...])
    o_ref[...] = (_sigmoid(ga_ref[...]) * ya + _sigmoid(gb_ref[...]) * yb).astype(o_ref.dtype)


def merge_branches(att, rwk, gates, wa, wb, tm, tn):
    m = att.shape[0]
    n = wa.shape[1]
    nb = n // tn
    return pl.pallas_call(
        _merge_kernel,
        grid=(m // tm, nb),
        in_specs=[pl.BlockSpec((tm, WIDTH_A), lambda i, j: (i, 0)),
                  pl.BlockSpec((tm, WIDTH_B), lambda i, j: (i, 0)),
                  pl.BlockSpec((tm, tn), lambda i, j: (i, j)),
                  pl.BlockSpec((tm, tn), lambda i, j: (i, j + nb)),
                  pl.BlockSpec((WIDTH_A, tn), lambda i, j: (0, j)),
                  pl.BlockSpec((WIDTH_B, tn), lambda i, j: (0, j))],
        out_specs=pl.BlockSpec((tm, tn), lambda i, j: (i, j)),
        out_shape=jax.ShapeDtypeStruct((m, n), BF16),
        compiler_params=_cparams(("parallel", "arbitrary")),
        name="merge_branches",
    )(att, rwk, gates, gates, wa, wb)


def _mm_norm_res_kernel(*refs, next_norm):
    if next_norm:
        a_ref, w_ref, x_ref, g_ref, g_next_ref, o_ref, h_ref, acc_ref = refs
    else:
        a_ref, w_ref, x_ref, g_ref, o_ref, acc_ref = refs
    kk = pl.program_id(1)

    @pl.when(kk == 0)
    def _():
        acc_ref[...] = jnp.zeros(acc_ref.shape, F32)

    acc_ref[...] += _dot(a_ref[...], w_ref[...])

    @pl.when(kk == pl.num_programs(1) - 1)
    def _():
        y = x_ref[...] + _rms(acc_ref[...], g_ref[...], NORM_EPS)
        o_ref[...] = y
        if next_norm:
            h_ref[...] = _rms(y, g_next_ref[...], NORM_EPS).astype(h_ref.dtype)


def matmul_norm_residual(a, w, x, g, tm, tk, name, g_next=None):
    m, kdim = a.shape
    n = w.shape[1]
    vec = pl.BlockSpec((1, n), lambda i, k: (0, 0))
    rows = pl.BlockSpec((tm, n), lambda i, k: (i, 0))
    next_norm = g_next is not None
    return pl.pallas_call(
        functools.partial(_mm_norm_res_kernel, next_norm=next_norm),
        grid=(m // tm, kdim // tk),
        in_specs=[pl.BlockSpec((tm, tk), lambda i, k: (i, k)),
                  pl.BlockSpec((tk, n), lambda i, k: (k, 0)),
                  rows, vec] + ([vec] if next_norm else []),
        out_specs=[rows, rows] if next_norm else rows,
        out_shape=([jax.ShapeDtypeStruct((m, n), F32), jax.ShapeDtypeStruct((m, n), BF16)] if next_norm
                   else jax.ShapeDtypeStruct((m, n), F32)),
        scratch_shapes=[pltpu.VMEM((tm, n), F32)],
        compiler_params=_cparams(("parallel", "arbitrary")),
        name=name,
    )(a, w, x, g.reshape(1, n), *([g_next.reshape(1, n)] if next_norm else []))


def _gate_up_kernel(h_ref, wg_ref, wu_ref, o_ref, wg16_ref, wu16_ref):
    @pl.when(pl.program_id(1) == 0)
    def _():
        wg16_ref[...] = wg_ref[...].astype(BF16)
        wu16_ref[...] = wu_ref[...].astype(BF16)

    h = h_ref[...]
    gt = _dot(h, wg16_ref[...])
    up = _dot(h, wu16_ref[...])
    o_ref[...] = (gt * _sigmoid(gt) * up).astype(o_ref.dtype)


def ffn_gate_up(h, wg, wu, tm, tn):
    m, kdim = h.shape
    n = wg.shape[1]
    return pl.pallas_call(
        _gate_up_kernel,
        grid=(n // tn, m // tm),
        in_specs=[pl.BlockSpec((tm, kdim), lambda j, i: (i, 0)),
                  pl.BlockSpec((kdim, tn), lambda j, i: (0, j)),
                  pl.BlockSpec((kdim, tn), lambda j, i: (0, j))],
        out_specs=pl.BlockSpec((tm, tn), lambda j, i: (i, j)),
        out_shape=jax.ShapeDtypeStruct((m, n), BF16),
        scratch_shapes=[pltpu.VMEM((kdim, tn), BF16), pltpu.VMEM((kdim, tn), BF16)],
        compiler_params=_cparams(("parallel", "arbitrary")),
        name="ffn_gate_up",
    )(h, wg, wu)


def _pad_lora_cols(x, axis=-1):
    o = 3 * WIDTH_B
    x = jnp.moveaxis(x, axis, -1)
    z = jnp.zeros(x.shape[:-1] + (LORA_PAD - R_DECAY,), x.dtype)
    x = jnp.concatenate([x[..., :o], x[..., o:o + R_DECAY], z,
                         x[..., o + R_DECAY:o + R_DECAY + R_AAA], z,
                         x[..., o + R_DECAY + R_AAA:]], axis=-1)
    return jnp.moveaxis(x, -1, axis)


def _unpad_lora_cols(x):
    o = 3 * WIDTH_B
    return jnp.concatenate([x[..., :o], x[..., o:o + R_DECAY], x[..., o + LORA_PAD:o + LORA_PAD + R_AAA],
                            x[..., o + 2 * LORA_PAD:]], axis=-1)


def _pad_rows(w, rows):
    return jnp.concatenate([w, jnp.zeros((rows - w.shape[0],) + w.shape[1:], w.dtype)], axis=0)


def _tiles(m):
    return dict(
        tm=min(m, 1024),
        tm_norm=min(m, 512),
        tm_prep=min(m, 256),
        tm_out=min(m, 512),
        tn=1024,
        tn_rw=896,
        tn_ffn=512,
        tk_out=D_MODEL,
        tk_ffn=2816,
        rwkv_pairs=N_HEADS_B // 2,
    )


def _group(x, pos, prev_rows, s0, n_seq, wts, tiles, q_dtype, q_scale, attention_fn):
    tm = tiles["tm"]
    h = rmsnorm_bf16(x, wts["norm_mix_pre"], tiles["tm_norm"])
    tables = rope_tables(pos)
    w_in, tn = wts["w_in"], tiles["tn"]
    (q,) = projection(h, w_in, 0, WIDTH_A, [q_dtype], tm, tn, tables, q_scale, name="proj_q")
    one_sequence = prev_rows is None
    k, *k16 = projection(h, w_in, WIDTH_A, WIDTH_A, [F32, BF16] if one_sequence else [F32], tm, tn, tables,
                         name="proj_k")
    v, *v_t = projection(h, w_in, 2 * WIDTH_A, WIDTH_A, [F32], tm, tn, values_t=one_sequence, name="proj_v")
    (rw,) = projection(h, wts["w_rw"], 0, RW_PAD_COLS, [F32], tm, tiles["tn_rw"], name="proj_rw")
    (gates,) = projection(h, wts["w_gates"], 0, 2 * D_MODEL, [F32], tm, tn, name="proj_gates")

    att = attention_fn(q, k, v, k16, v_t)

    prep = rwkv_prep(rw, prev_rows, wts["mu"], wts["w0"], wts["a0"], wts["k_k"], wts["k_a"],
                     wts["w_decay"], wts["w_aaa"], wts["w_gate"], tiles["tm_prep"])
    if one_sequence:
        rwk, s_bd = rwkv_chunked(*prep, wts["lnx_w"], wts["lnx_b"], wts["r_k"], _pair_states_to_bd(s0), n_seq,
                                 tiles["rwkv_pairs"])
    else:
        rwk, s_bd = rwkv_step(*prep, wts["lnx_w"], wts["lnx_b"], wts["r_k"], _pair_states_to_bd(s0), SUBLANES)
    s_new = _bd_to_pair_states(s_bd)

    mrg = merge_branches(att, rwk, gates, wts["w_branch_a"], wts["w_branch_b"], tm, tiles["tn"])
    x1, hf = matmul_norm_residual(mrg, wts["w_out"], x, wts["norm_mix_post"], tiles["tm_out"], tiles["tk_out"],
                                  "out_proj", g_next=wts["norm_ffn_pre"])
    act = ffn_gate_up(hf, wts["w_ffn_gate"], wts["w_ffn_up"], tm, tiles["tn_ffn"])
    y = matmul_norm_residual(act, wts["w_ffn_down"], x1, wts["norm_ffn_post"], tiles["tm_out"], tiles["tk_ffn"], "ffn_down")
    return y, k, v, rw, s_new


def kernel(x_prompt, x_sample, cache_k, cache_v, state_rwkv, state_shift, page_table, norm_mix_pre, w_in, lambda_q1, lambda_k1, lambda_q2, lambda_k2, subln_w, rw_mu, w0, w_decay, a0, w_aaa, w_gate_lora, k_k, k_a, r_k, lnx_w, lnx_b, w_branch_a, w_branch_b, w_out, norm_mix_post, norm_ffn_pre, w_ffn_gate, w_ffn_up, w_ffn_down, norm_ffn_post):
    depth = w_in.shape[0]
    bsz, seq = x_prompt.shape[0], x_prompt.shape[1]
    db, ds = x_sample.shape[0], x_sample.shape[1]
    assert depth == 1 and bsz == 1 and ds == 1
    n_pool = cache_k.shape[1]
    n_pages = page_table.shape[1]
    past = n_pages * PAGE_SIZE
    l = 0
    lambda_init = 0.8 - 0.6 * math.exp(-0.3 * l)

    wi = w_in[l].T
    row = lambda p: p.reshape(1, -1)
    wts = {
        "norm_mix_pre": norm_mix_pre[l],
        "w_in": wi,
        "w_rw": _pad_lora_cols(wi[A_COLS:A_COLS + RW_COLS], axis=0),
        "w_gates": wi[A_COLS + RW_COLS:],
        "mu": _pad_lora_cols(row(rw_mu[l])),
        "w0": row(w0[l]), "a0": row(a0[l]), "k_k": row(k_k[l]), "k_a": row(k_a[l]),
        "w_decay": _pad_rows(w_decay[l], LORA_PAD).astype(BF16),
        "w_aaa": _pad_rows(w_aaa[l], LORA_PAD).astype(BF16),
        "w_gate": w_gate_lora[l].astype(BF16),
        "lnx_w": lnx_w[l], "lnx_b": lnx_b[l], "r_k": r_k[l],
        "w_branch_a": w_branch_a[l].astype(BF16), "w_branch_b": w_branch_b[l].astype(BF16),
        "w_out": w_out[l].astype(BF16), "norm_mix_post": norm_mix_post[l],
        "norm_ffn_pre": norm_ffn_pre[l],
        "w_ffn_gate": w_ffn_gate[l], "w_ffn_up": w_ffn_up[l],
        "w_ffn_down": w_ffn_down[l].astype(BF16), "norm_ffn_post": norm_ffn_post[l],
    }
    lam_params = [row(p[l]) for p in (lambda_q1, lambda_k1, lambda_q2, lambda_k2)]
    subln = subln_w[l]

    tiles_p = _tiles(seq)

    def prompt_attention(q, k, v, k16, v_t):
        del k, v
        return prompt_diff_attention(q, k16[0], v_t[0], lam_params, subln, lambda_init, 512, 512)

    s0_p = jnp.zeros((bsz, N_HEADS_B, D_HEAD_B, D_HEAD_B), F32)
    y_p, k_p, v_p, rw_p, s_p = _group(x_prompt.reshape(seq, D_MODEL), jnp.arange(seq), None, s0_p, 1,
                                      wts, tiles_p, BF16, LOG2_E * D_HEAD_A ** -0.5, prompt_attention)

    tiles_s = _tiles(db)
    ck = jnp.transpose(cache_k[l], (0, 2, 3, 1)).reshape(n_pool, WIDTH_A, PAGE_SIZE)
    cv = cache_v[l]

    def sample_attention(q, k, v, k16, v_t):
        del k16, v_t
        return sample_diff_attention(q, k, v, ck, cv, page_table, lam_params, subln, lambda_init, 8)

    y_s, k_s, v_s, rw_s, s_s = _group(x_sample.reshape(db, D_MODEL), jnp.full((db,), past, jnp.int32),
                                      _pad_lora_cols(state_shift[l]),
                                      state_rwkv[l], db, wts, tiles_s, F32, D_HEAD_A ** -0.5, sample_attention)

    return (y_p.reshape(bsz, seq, D_MODEL),
            y_s.reshape(db, ds, D_MODEL),
            k_p.reshape(1, bsz, seq, 2 * N_HEADS_A, D_HEAD_A),
            v_p.reshape(1, bsz, seq, N_HEADS_A, 2 * D_HEAD_A),
            s_p.reshape(1, bsz, N_HEADS_B, D_HEAD_B, D_HEAD_B),
            _unpad_lora_cols(rw_p[-1:]).reshape(1, bsz, RW_COLS),
            k_s.reshape(1, db, ds, 2 * N_HEADS_A, D_HEAD_A),
            v_s.reshape(1, db, ds, N_HEADS_A, 2 * D_HEAD_A),
            s_s.reshape(1, db, N_HEADS_B, D_HEAD_B, D_HEAD_B),
            _unpad_lora_cols(rw_s).reshape(1, db, RW_COLS))
```

```python
import functools
import math

import jax
import jax.numpy as jnp
from jax import lax
from jax.experimental import pallas as pl
from jax.experimental.pallas import tpu as pltpu

F32 = jnp.float32
BF16 = jnp.bfloat16

D_MODEL = 2048
N_HEADS_A = 8
D_HEAD_A = 64
WIDTH_A = 2 * N_HEADS_A * D_HEAD_A
N_HEADS_B = 16
D_HEAD_B = 64
WIDTH_B = N_HEADS_B * D_HEAD_B
R_DECAY = 96
R_AAA = 96
R_GATE = 256
ROPE_THETA = 10000.0
NORM_EPS = 1e-6
SUBLN_EPS = 1e-5
GN_EPS = 64e-5
LOG2_E = math.log2(math.e)
KK_NORM_FLOOR_SQ = 1e-24
PAGE_SIZE = 128
A_COLS = 3 * WIDTH_A
RW_COLS = 3 * WIDTH_B + R_DECAY + R_AAA + R_GATE

LANES = 128
SUBLANES = 8
ONES_ROWS = 16
KV_BLOCKS_IN_FLIGHT = 2
LORA_PAD = 128
RW_PAD_COLS = 3 * WIDTH_B + 2 * LORA_PAD + R_GATE
CHUNK = 64
INV_BLOCK = 16
VMEM_LIMIT = 56 * 1024 * 1024


def _cparams(sem):
    return pltpu.CompilerParams(dimension_semantics=sem, vmem_limit_bytes=VMEM_LIMIT)


def _dot(a, b):
    return jnp.dot(a, b, preferred_element_type=F32)


def _dot_nt(a, b):
    return lax.dot_general(a, b, (((1,), (1,)), ((), ())), preferred_element_type=F32)


def _dot_tn(a, b):
    return lax.dot_general(a, b, (((0,), (0,)), ((), ())), preferred_element_type=F32)


def _sigmoid(x):
    return 0.5 * jnp.tanh(0.5 * x) + 0.5


def _rms(x, g, eps):
    return x * lax.rsqrt(jnp.mean(x * x, axis=-1, keepdims=True) + eps) * g


def _norm_kernel(x_ref, g_ref, o_ref):
    o_ref[...] = _rms(x_ref[...], g_ref[...], NORM_EPS).astype(o_ref.dtype)


def rmsnorm_bf16(x, g, tm):
    m, d = x.shape
    return pl.pallas_call(
        _norm_kernel,
        grid=(m // tm,),
        in_specs=[pl.BlockSpec((tm, d), lambda i: (i, 0)),
                  pl.BlockSpec((1, d), lambda i: (0, 0))],
        out_specs=pl.BlockSpec((tm, d), lambda i: (i, 0)),
        out_shape=jax.ShapeDtypeStruct((m, d), BF16),
        compiler_params=_cparams(("parallel",)),
        name="rmsnorm_bf16",
    )(x, g.reshape(1, d))


def _rope_tile(x, cos, sin_signed):
    lane = lax.broadcasted_iota(jnp.int32, x.shape, 1)
    first_half = (lane % D_HEAD_A) < (D_HEAD_A // 2)
    partner = jnp.where(first_half,
                        pltpu.roll(x, LANES - D_HEAD_A // 2, 1),
                        pltpu.roll(x, D_HEAD_A // 2, 1))
    return x * cos + partner * sin_signed


def _proj_kernel(*refs, rope, scale, n_out, values_t):
    if rope:
        a_ref, w_ref, cos_ref, sin_ref = refs[:4]
        outs = refs[4:-1]
    else:
        a_ref, w_ref = refs[:2]
        outs = refs[2:-1]
    w16_ref = refs[-1]

    @pl.when(pl.program_id(1) == 0)
    def _():
        w16_ref[...] = w_ref[...].astype(BF16)

    acc = _dot_nt(a_ref[...], w16_ref[...])
    if rope:
        cos = cos_ref[...]
        sin = sin_ref[...]
        tn = acc.shape[1]
        acc = jnp.concatenate(
            [_rope_tile(acc[:, c * LANES:(c + 1) * LANES], cos, sin) for c in range(tn // LANES)], axis=1)
    if scale != 1.0:
        acc = acc * scale
    for o_ref in outs[:n_out]:
        o_ref[...] = acc.astype(o_ref.dtype)
    if values_t:
        t_ref = outs[n_out]
        acc_t = jnp.transpose(acc)
        for grp in range(acc.shape[1] // LANES):
            t_ref[grp, 0:LANES, :] = acc_t[grp * LANES:(grp + 1) * LANES, :].astype(t_ref.dtype)
            t_ref[grp, LANES:, :] = jnp.ones((ONES_ROWS, acc.shape[0]), t_ref.dtype)


def projection(a, w, col0, n, out_dtypes, tm, tn, rope_tables=None, scale=1.0, values_t=False,
               name="projection"):
    m, kdim = a.shape
    j0 = col0 // tn
    in_specs = [pl.BlockSpec((tm, kdim), lambda j, i: (i, 0)),
                pl.BlockSpec((tn, kdim), lambda j, i: (j + j0, 0))]
    args = [a, w]
    if rope_tables is not None:
        in_specs += [pl.BlockSpec((tm, LANES), lambda j, i: (i, 0))] * 2
        args += list(rope_tables)
    out_specs = [pl.BlockSpec((tm, tn), lambda j, i: (i, j)) for _ in out_dtypes]
    out_shape = [jax.ShapeDtypeStruct((m, n), dt) for dt in out_dtypes]
    if values_t:
        rows = LANES + ONES_ROWS
        out_specs.append(pl.BlockSpec((tn // LANES, rows, tm), lambda j, i: (j, 0, i)))
        out_shape.append(jax.ShapeDtypeStruct((n // LANES, rows, m), BF16))
    outs = pl.pallas_call(
        functools.partial(_proj_kernel, rope=rope_tables is not None, scale=scale, n_out=len(out_dtypes),
                          values_t=values_t),
        grid=(n // tn, m // tm),
        in_specs=in_specs,
        out_specs=out_specs,
        out_shape=out_shape,
        scratch_shapes=[pltpu.VMEM((tn, kdim), BF16)],
        compiler_params=_cparams(("parallel", "arbitrary")),
        name=name,
    )(*args)
    return outs


def rope_tables(pos):
    half = D_HEAD_A // 2
    inv = 1.0 / (ROPE_THETA ** (jnp.arange(half, dtype=F32) / half))
    ang = pos.astype(F32)[:, None] * inv[None, :]
    cos = jnp.cos(ang)
    sin = jnp.sin(ang)
    cos2 = jnp.concatenate([cos, cos, cos, cos], axis=1)
    sin2 = jnp.concatenate([-sin, sin, -sin, sin], axis=1)
    return cos2, sin2


def _diff_lambda(lq1, lk1, lq2, lk2, lambda_init):
    s1 = jnp.sum(lq1 * lk1, axis=-1, keepdims=True)
    s2 = jnp.sum(lq2 * lk2, axis=-1, keepdims=True)
    return jnp.exp(s1) - jnp.exp(s2) + lambda_init


def _flash_kernel(q_ref, k_ref, v_ref, lq1_ref, lk1_ref, lq2_ref, lk2_ref, g_ref, o_ref,
                  qs_ref, m_ref, acc_ref, s_ref, p_ref, *, tq, tk, col_chunk, row_chunk, lambda_init):
    qi = pl.program_id(1)
    ratio = tq // tk
    qt = jnp.transpose(q_ref[...].astype(F32))
    qt2 = jnp.concatenate([qt, qt], axis=1)
    row = lax.broadcasted_iota(jnp.int32, qt2.shape, 0)
    col = lax.broadcasted_iota(jnp.int32, qt2.shape, 1)
    qs_ref[...] = jnp.where((row < D_HEAD_A) == (col < tq), qt2, 0.0).astype(BF16)
    m_ref[...] = jnp.full(m_ref.shape, -jnp.inf, F32)
    acc_ref[...] = jnp.zeros(acc_ref.shape, F32)

    def column_chunk(c, slot, k_blk, v_t, kv, masked):
        cols = slice(c * col_chunk, (c + 1) * col_chunk)
        rows = tk
        if masked:
            rows = max(0, min(tk, (c * col_chunk) % tq + col_chunk - slot * tk))
            if rows == 0:
                return
        s_ref[slot, 0:rows, cols] = _dot(k_blk[0:rows], qs_ref[:, cols])
        yield
        sub = SUBLANES
        groups = row_chunk // sub

        def scores(r0):
            blk = s_ref[slot, r0:r0 + row_chunk, cols]
            if masked:
                krow = lax.broadcasted_iota(jnp.int32, blk.shape, 0) + r0
                qcol = lax.broadcasted_iota(jnp.int32, blk.shape, 1) + (c * col_chunk) % tq
                blk = jnp.where(kv * tk + krow <= qi * tq + qcol, blk, -jnp.inf)
            return blk

        part = None
        for r0 in range(0, rows, row_chunk):
            mx = jnp.max(scores(r0).reshape(groups, sub, col_chunk), axis=0)
            part = mx if part is None else jnp.maximum(part, mx)
        m_prev = m_ref[:, cols]
        m_new = jnp.maximum(m_prev, jnp.max(part, axis=0, keepdims=True))
        alpha = jnp.exp2(m_prev - m_new)
        for r0 in range(0, rows, row_chunk):
            p_ref[slot, r0:r0 + row_chunk, cols] = jnp.exp2(scores(r0) - m_new).astype(BF16)
        m_ref[:, cols] = m_new
        pv = _dot(v_t[:, 0:rows], p_ref[slot, 0:rows, cols])
        yield
        acc_ref[:, cols] = alpha * acc_ref[:, cols] + pv

    def blocks(kvs, masked):
        chunks = []
        for slot, kv in enumerate(kvs):
            start = pl.multiple_of(kv * tk, tk)
            k_blk = k_ref[pl.ds(start, tk), :]
            v_t = v_ref[0, :, pl.ds(start, tk)]
            chunks += [column_chunk(c, slot, k_blk, v_t, kv, masked) for c in range(2 * tq // col_chunk)]
        _run_interleaved(chunks)

    n_full = qi * ratio
    slots = s_ref.shape[0]

    def full_blocks(j, carry):
        blocks([j * slots + s for s in range(slots)], False)
        return carry

    lax.fori_loop(0, n_full // slots, full_blocks, 0)
    for rem in range(1, slots):
        @pl.when(n_full % slots == rem)
        def _(rem=rem):
            blocks([n_full - rem + s for s in range(rem)], False)
    blocks([n_full + r for r in range(ratio)], True)

    lam = _diff_lambda(lq1_ref[...], lk1_ref[...], lq2_ref[...], lk2_ref[...], lambda_init)
    o = acc_ref[0:LANES, :] * (1.0 / acc_ref[LANES:LANES + 1, :])
    o = jnp.transpose(o[:, 0:tq] - lam * o[:, tq:2 * tq])
    o = _rms(o, g_ref[...], SUBLN_EPS) * (1.0 - lambda_init)
    o_ref[...] = o.astype(o_ref.dtype)


def prompt_diff_attention(q, k, v_t, lam_params, subln_w, lambda_init, tq, tk):
    t = q.shape[0]
    n_pairs = N_HEADS_A
    assert tq % tk == 0 and tq // tk <= KV_BLOCKS_IN_FLIGHT
    small = pl.BlockSpec((1, D_HEAD_A), lambda h, i: (0, 0))
    whole = pl.BlockSpec((t, LANES), lambda h, i: (0, h))
    whole_t = pl.BlockSpec((1, LANES + ONES_ROWS, t), lambda h, i: (h, 0, 0))
    return pl.pallas_call(
        functools.partial(_flash_kernel, tq=tq, tk=tk, col_chunk=2 * LANES, row_chunk=2 * SUBLANES,
                          lambda_init=lambda_init),
        grid=(n_pairs, t // tq),
        in_specs=[pl.BlockSpec((tq, LANES), lambda h, i: (i, h)),
                  whole, whole_t,
                  small, small, small, small,
                  pl.BlockSpec((1, LANES), lambda h, i: (0, 0))],
        out_specs=pl.BlockSpec((tq, LANES), lambda h, i: (i, h)),
        out_shape=jax.ShapeDtypeStruct((t, WIDTH_A), BF16),
        scratch_shapes=[pltpu.VMEM((LANES, 2 * tq), BF16),
                        pltpu.VMEM((1, 2 * tq), F32),
                        pltpu.VMEM((LANES + ONES_ROWS, 2 * tq), F32),
                        pltpu.VMEM((KV_BLOCKS_IN_FLIGHT, tk, 2 * tq), F32),
                        pltpu.VMEM((KV_BLOCKS_IN_FLIGHT, tk, 2 * tq), BF16)],
        compiler_params=_cparams(("parallel", "arbitrary")),
        name="prompt_diff_attention",
    )(q, k, v_t, *lam_params, subln_w.reshape(1, LANES))


def _decode_kernel(pt_ref, q_ref, kn_ref, vn_ref, lq1_ref, lk1_ref, lq2_ref, lk2_ref, g_ref, *rest,
                   pages_per_step, lambda_init):
    del pt_ref
    k_refs = rest[:pages_per_step]
    v_refs = rest[pages_per_step:2 * pages_per_step]
    o_ref = rest[2 * pages_per_step]
    qm_ref, ex_ref, m_ref, l_ref, acc_ref = rest[2 * pages_per_step + 1:]
    hp = N_HEADS_A
    n_rows = 2 * hp
    j = pl.program_id(1)

    def pair_mask():
        row = lax.broadcasted_iota(jnp.int32, (n_rows, PAGE_SIZE * hp), 0)
        lane = lax.broadcasted_iota(jnp.int32, (n_rows, PAGE_SIZE * hp), 1)
        return (lane % hp) == (row % hp)

    @pl.when(j == 0)
    def _():
        row = lax.broadcasted_iota(jnp.int32, (n_rows, WIDTH_A), 0)
        lane = lax.broadcasted_iota(jnp.int32, (n_rows, WIDTH_A), 1)
        qb = jnp.broadcast_to(q_ref[0], (n_rows, WIDTH_A))
        qm_ref[...] = jnp.where(lane // D_HEAD_A == 2 * (row % hp) + row // hp, qb, 0.0)
        tok = lax.broadcasted_iota(jnp.int32, ex_ref.shape, 0)
        pos = lax.broadcasted_iota(jnp.int32, ex_ref.shape, 1)
        ex_ref[...] = jnp.where(pos // hp == tok, 1.0, 0.0).astype(BF16)
        m_ref[...] = jnp.full(m_ref.shape, -jnp.inf, F32)
        l_ref[...] = jnp.zeros(l_ref.shape, F32)
        acc_ref[...] = jnp.zeros(acc_ref.shape, F32)

    qm = qm_ref[...].astype(BF16)
    keep = pair_mask()
    pages = range(pages_per_step)
    s = jnp.concatenate([_dot(qm, k_refs[i][0].astype(BF16)) for i in pages], axis=1)
    m_prev = m_ref[...]
    m_new = jnp.maximum(m_prev, jnp.max(s, axis=-1, keepdims=True))
    alpha = jnp.exp(m_prev - m_new)
    p = jnp.exp(s - m_new)
    l_ref[...] = alpha * l_ref[...] + jnp.sum(p, axis=-1, keepdims=True)
    p16 = p.astype(BF16)
    p_wide = [jnp.where(keep, _dot(p16[:, i * PAGE_SIZE:(i + 1) * PAGE_SIZE], ex_ref[...]), 0.0).astype(BF16)
              for i in pages]
    pv = _dot(p_wide[0], v_refs[0][0].reshape(PAGE_SIZE * hp, LANES).astype(BF16))
    for i in pages[1:]:
        pv += _dot(p_wide[i], v_refs[i][0].reshape(PAGE_SIZE * hp, LANES).astype(BF16))
    acc_ref[...] = alpha * acc_ref[...] + pv
    m_ref[...] = m_new

    @pl.when(j == pl.num_programs(1) - 1)
    def _():
        s = jnp.sum(qm_ref[...] * kn_ref[0], axis=-1, keepdims=True)
        m_prev = m_ref[...]
        m_new = jnp.maximum(m_prev, s)
        alpha = jnp.exp(m_prev - m_new)
        p = jnp.exp(s - m_new)
        l = alpha * l_ref[...] + p
        vn = vn_ref[0]
        acc = alpha * acc_ref[...] + p * jnp.concatenate([vn, vn], axis=0)
        lam = _diff_lambda(lq1_ref[...], lk1_ref[...], lq2_ref[...], lk2_ref[...], lambda_init)
        o = acc / l
        o = o[0:hp, :] - lam * o[hp:n_rows, :]
        o_ref[0] = (_rms(o, g_ref[...], SUBLN_EPS) * (1.0 - lambda_init)).astype(o_ref.dtype)


def sample_diff_attention(q, k_new, v_new, cache_k, cache_v, page_table, lam_params, subln_w,
                          lambda_init, pages_per_step):
    b = q.shape[0]
    n_pages = page_table.shape[1]
    steps = n_pages // pages_per_step
    hp = N_HEADS_A
    row3 = lambda x: x.reshape(b, 1, WIDTH_A)
    vec = pl.BlockSpec((1, 1, WIDTH_A), lambda bi, j, pt: (bi, 0, 0))
    small = pl.BlockSpec((1, D_HEAD_A), lambda bi, j, pt: (0, 0))
    pairs = pl.BlockSpec((1, hp, LANES), lambda bi, j, pt: (bi, 0, 0))

    def page_spec(i, shape):
        zeros = (0,) * len(shape)
        return pl.BlockSpec((1,) + shape,
                            lambda bi, j, pt: (pt[bi * n_pages + j * pages_per_step + i],) + zeros)

    grid_spec = pltpu.PrefetchScalarGridSpec(
        num_scalar_prefetch=1,
        grid=(b, steps),
        in_specs=[vec, vec, pairs, small, small, small, small,
                  pl.BlockSpec((1, LANES), lambda bi, j, pt: (0, 0))]
                 + [page_spec(i, (WIDTH_A, PAGE_SIZE)) for i in range(pages_per_step)]
                 + [page_spec(i, (PAGE_SIZE, hp, LANES)) for i in range(pages_per_step)],
        out_specs=pairs,
        scratch_shapes=[pltpu.VMEM((2 * hp, WIDTH_A), F32),
                        pltpu.VMEM((PAGE_SIZE, PAGE_SIZE * hp), BF16),
                        pltpu.VMEM((2 * hp, 1), F32),
                        pltpu.VMEM((2 * hp, 1), F32),
                        pltpu.VMEM((2 * hp, LANES), F32)],
    )
    out = pl.pallas_call(
        functools.partial(_decode_kernel, pages_per_step=pages_per_step, lambda_init=lambda_init),
        grid_spec=grid_spec,
        out_shape=jax.ShapeDtypeStruct((b, hp, LANES), BF16),
        compiler_params=_cparams(("parallel", "arbitrary")),
        name="sample_diff_attention",
    )(page_table.reshape(-1), row3(q), row3(k_new), v_new.reshape(b, hp, LANES), *lam_params,
      subln_w.reshape(1, LANES), *([cache_k] * pages_per_step), *([cache_v] * pages_per_step))
    return out.reshape(b, WIDTH_A)


def _rwkv_prep_kernel(c_ref, p_ref, mu_ref, w0_ref, a0_ref, kk_ref, ka_ref, wdec_ref, waaa_ref, wgate_ref,
                      r_ref, lw_ref, k_ref, v_ref, kkr_ref, a_ref, g_ref, *, one_sequence):
    c = c_ref[...]
    if one_sequence:
        above = jnp.where(pl.program_id(0) == 0, 0.0, p_ref[SUBLANES - 1:SUBLANES, :])
        row = lax.broadcasted_iota(jnp.int32, c.shape, 0)
        prev = jnp.where(row == 0, above, pltpu.roll(c, 1, 0))
    else:
        prev = p_ref[...]
    u = c + mu_ref[...] * (prev - c)
    wb = WIDTH_B
    r = u[:, 0:wb]
    kx = u[:, wb:2 * wb]
    v = u[:, 2 * wb:3 * wb]
    o = 3 * wb
    wd = u[:, o:o + LORA_PAD]
    ad = u[:, o + LORA_PAD:o + 2 * LORA_PAD]
    gd = u[:, o + 2 * LORA_PAD:]
    z = w0_ref[...] + _dot(jnp.tanh(wd).astype(BF16), wdec_ref[...])
    nz = -z
    softplus = jnp.maximum(nz, 0.0) + jnp.log(1.0 + jnp.exp(-jnp.abs(nz)))
    w_raw = -softplus - 0.5
    a = _sigmoid(a0_ref[...] + _dot(ad.astype(BF16), waaa_ref[...]))
    r_ref[...] = r
    lw_ref[...] = -jnp.exp(w_raw)
    k_ref[...] = kx * (1.0 + (a - 1.0) * ka_ref[...])
    v_ref[...] = v
    kkr_ref[...] = kx * kk_ref[...]
    a_ref[...] = a
    g_ref[...] = _dot(_sigmoid(gd).astype(BF16), wgate_ref[...])


def rwkv_prep(cols, prev_rows, mu, w0, a0, k_k, k_a, w_decay, w_aaa, w_gate, tm):
    m = cols.shape[0]
    wb = WIDTH_B
    row = lambda n: pl.BlockSpec((1, n), lambda i: (0, 0))
    full = lambda r_, c_: pl.BlockSpec((r_, c_), lambda i: (0, 0))
    tile = lambda n: pl.BlockSpec((tm, n), lambda i: (i, 0))
    one_sequence = prev_rows is None
    if one_sequence:
        prev_rows = cols
        prev_spec = pl.BlockSpec((SUBLANES, RW_PAD_COLS),
                                 lambda i: (jnp.maximum(i * (tm // SUBLANES) - 1, 0), 0))
    else:
        prev_spec = tile(RW_PAD_COLS)
    return pl.pallas_call(
        functools.partial(_rwkv_prep_kernel, one_sequence=one_sequence),
        grid=(m // tm,),
        in_specs=[tile(RW_PAD_COLS), prev_spec, row(RW_PAD_COLS), row(wb), row(wb), row(wb), row(wb),
                  full(LORA_PAD, wb), full(LORA_PAD, wb), full(R_GATE, wb)],
        out_specs=[tile(wb)] * 7,
        out_shape=[jax.ShapeDtypeStruct((m, wb), F32)] * 7,
        compiler_params=_cparams(("parallel",)),
        name="rwkv_prep",
    )(cols, prev_rows, mu, w0, a0, k_k, k_a, w_decay, w_aaa, w_gate)


def _rwkv_pair_chunk(r, lw, k, v, kkr, a, g, lnw, lnb, rk, s_prev):
    cl = CHUNK
    n = D_HEAD_B

    def stack(x):
        x2 = jnp.concatenate([x, x], axis=0)
        row = lax.broadcasted_iota(jnp.int32, x2.shape, 0)
        lane = lax.broadcasted_iota(jnp.int32, x2.shape, 1)
        return jnp.where((row // cl) == (lane // n), x2, 0.0)

    def fold(x_bd):
        return x_bd[0:cl, :] + x_bd[cl:2 * cl, :]

    row_c = lax.broadcasted_iota(jnp.int32, lw.shape, 0)
    cum = lw
    shift = 1
    while shift < cl:
        cum = cum + jnp.where(row_c >= shift, pltpu.roll(cum, shift, 0), 0.0)
        shift *= 2
    cum_last = cum[cl - 1:cl, :]

    kk_bd = stack(kkr)
    kk = fold(kk_bd * lax.rsqrt(jnp.maximum(jnp.sum(kk_bd * kk_bd, axis=-1, keepdims=True), KK_NORM_FLOOR_SQ)))
    b = kk * a

    e_pos = jnp.exp(cum)
    e_neg = jnp.exp(-cum)
    e_rest = jnp.exp(cum_last - cum)
    r_t = stack(r * e_pos)
    a_t = stack(-kk * jnp.exp(cum - lw))
    b_t = b * e_neg
    k_t = k * e_neg
    b_h = stack(b * e_rest)
    k_h = stack(k * e_rest)
    v_bd = stack(v)
    gamma = jnp.exp(cum_last)

    lhs = jnp.concatenate([a_t, r_t], axis=0).astype(BF16)
    rhs = jnp.concatenate([b_t, b_t, k_t, k_t], axis=0).astype(BF16)
    m1 = _dot_nt(lhs, rhs)
    xr = _dot_nt(lhs, s_prev.astype(BF16))
    yield
    c2 = 2 * cl
    row = lax.broadcasted_iota(jnp.int32, (c2, c2), 0)
    col = lax.broadcasted_iota(jnp.int32, (c2, c2), 1)
    same_head = (row // cl) == (col // cl)
    strict = jnp.logical_and(same_head, row > col)
    incl = jnp.logical_and(same_head, row >= col)
    a_ab = jnp.where(strict, m1[0:c2, 0:c2], 0.0)
    a_ak = jnp.where(strict, m1[0:c2, c2:2 * c2], 0.0)
    a_rb = jnp.where(incl, m1[c2:2 * c2, 0:c2], 0.0)
    a_rk = jnp.where(incl, m1[c2:2 * c2, c2:2 * c2], 0.0)

    mm = lambda x, y: _dot(x.astype(BF16), y.astype(BF16))
    eye = jnp.where(row == col, 1.0, 0.0)
    diag_blk = (row // INV_BLOCK) == (col // INV_BLOCK)
    x = jnp.where(diag_blk, a_ab, 0.0)
    a_off = a_ab - x
    t_d = eye + x
    vv = mm(jnp.concatenate([a_ak, a_rk], axis=0), v_bd)
    span = 2
    while span < INV_BLOCK:
        x = mm(x, x)
        yield
        t_d = t_d + mm(t_d, x)
        span *= 2
    yield
    nm = mm(t_d, a_off)
    yield
    nm_pow = nm
    m_acc = eye + nm
    blocks = 2
    while blocks < cl // INV_BLOCK:
        sq = mm(nm_pow, nm_pow)
        yield
        m_acc = m_acc + mm(m_acc, sq)
        yield
        nm_pow = sq
        blocks *= 2
    t_inv = mm(m_acc, t_d)
    yield
    u = mm(t_inv, xr[0:c2, :] + vv[0:c2, :])
    yield
    y_bd = xr[c2:2 * c2, :] + mm(a_rb, u) + vv[c2:2 * c2, :]
    s_new = s_prev * gamma + _dot_tn(jnp.concatenate([u, v_bd], axis=0).astype(BF16),
                                     jnp.concatenate([b_h, k_h], axis=0).astype(BF16))
    yield

    lane2 = lax.broadcasted_iota(jnp.int32, (c2, LANES), 1)
    row2 = lax.broadcasted_iota(jnp.int32, (c2, LANES), 0)
    head_mask = (row2 // cl) == (lane2 // n)
    mean = jnp.sum(y_bd, axis=-1, keepdims=True) * (1.0 / n)
    d = jnp.where(head_mask, y_bd - mean, 0.0)
    var = jnp.sum(d * d, axis=-1, keepdims=True) * (1.0 / n)
    yn = fold(d * lax.rsqrt(var + GN_EPS))
    yn = yn * lnw + lnb
    rk_sum = jnp.sum(stack(r * k * rk), axis=-1, keepdims=True)
    bonus = fold(rk_sum * v_bd)
    return (yn + bonus) * g, s_new


def _run_interleaved(generators):
    results = [None] * len(generators)
    active = list(enumerate(generators))
    while active:
        still = []
        for i, gen in active:
            try:
                next(gen)
                still.append((i, gen))
            except StopIteration as done:
                results[i] = done.value
        active = still
    return results


def _rwkv_chunk_kernel(r_ref, lw_ref, k_ref, v_ref, kkr_ref, a_ref, g_ref, lnw_ref, lnb_ref, rk_ref, s0_ref,
                       y_ref, s_out_ref, s_ref, *, pairs_per_step):
    c = pl.program_id(2)

    @pl.when(c == 0)
    def _():
        s_ref[...] = s0_ref[0]

    in_refs = (r_ref, lw_ref, k_ref, v_ref, kkr_ref, a_ref, g_ref, lnw_ref, lnb_ref, rk_ref)
    lanes = [slice(p * LANES, (p + 1) * LANES) for p in range(pairs_per_step)]
    loaded = [tuple(ref[:, ln] for ref in in_refs) + (s_ref[p],) for p, ln in enumerate(lanes)]
    results = _run_interleaved([_rwkv_pair_chunk(*args) for args in loaded])
    for p, (y, s_new) in enumerate(results):
        y_ref[:, lanes[p]] = y.astype(y_ref.dtype)
        s_ref[p] = s_new

    @pl.when(c == pl.num_programs(2) - 1)
    def _():
        s_out_ref[0] = s_ref[...]


def rwkv_chunked(r, lw, k, v, kkr, a, g, lnx_w, lnx_b, r_k, s0_bd, n_seq, pairs_per_step):
    m = r.shape[0]
    n_pairs = N_HEADS_B // 2
    nc = m // n_seq // CHUNK
    pps = pairs_per_step
    tile = pl.BlockSpec((CHUNK, pps * LANES), lambda s, h, c: (s * nc + c, h))
    par = pl.BlockSpec((1, pps * LANES), lambda s, h, c: (0, h))
    st = pl.BlockSpec((1, pps, LANES, LANES), lambda s, h, c: (s, h, 0, 0))
    return pl.pallas_call(
        functools.partial(_rwkv_chunk_kernel, pairs_per_step=pps),
        grid=(n_seq, n_pairs // pps, nc),
        in_specs=[tile] * 7 + [par] * 3 + [st],
        out_specs=[tile, st],
        out_shape=[jax.ShapeDtypeStruct((m, WIDTH_B), BF16),
                   jax.ShapeDtypeStruct((n_seq, n_pairs, LANES, LANES), F32)],
        scratch_shapes=[pltpu.VMEM((pps, LANES, LANES), F32)],
        compiler_params=_cparams(("parallel", "parallel", "arbitrary")),
        name="rwkv_chunked",
    )(r, lw, k, v, kkr, a, g, lnx_w.reshape(1, -1), lnx_b.reshape(1, -1), r_k.reshape(1, -1), s0_bd)


def _rwkv_step_kernel(r_ref, lw_ref, k_ref, v_ref, kkr_ref, a_ref, g_ref, lnw_ref, lnb_ref, rk_ref, s_ref,
                      y_ref, s_out_ref):
    n = D_HEAD_B
    seqs = r_ref.shape[0]
    lane = lax.broadcasted_iota(jnp.int32, (seqs, LANES), 1)
    first = lane < n
    sq_row = lax.broadcasted_iota(jnp.int32, (LANES, LANES), 0)
    sq_lane = lax.broadcasted_iota(jnp.int32, (LANES, LANES), 1)
    same_head = (sq_row // n) == (sq_lane // n)

    def head_sums(x):
        lo = jnp.sum(jnp.where(first, x, 0.0), axis=-1, keepdims=True)
        hi = jnp.sum(jnp.where(first, 0.0, x), axis=-1, keepdims=True)
        return jnp.where(first, lo, hi)

    def to_columns(x):
        return jnp.transpose(jnp.concatenate([x, jnp.zeros((LANES - seqs, LANES), F32)], axis=0))

    for p in range(s_ref.shape[1]):
        ln = slice(p * LANES, (p + 1) * LANES)
        r, k, v, a = r_ref[:, ln], k_ref[:, ln], v_ref[:, ln], a_ref[:, ln]
        kkr = kkr_ref[:, ln]
        kk = kkr * lax.rsqrt(jnp.maximum(head_sums(kkr * kkr), KK_NORM_FLOOR_SQ))
        b = kk * a
        decay = jnp.exp(lw_ref[:, ln])
        v_cols = to_columns(v)
        y_cols = jnp.zeros((LANES, LANES), F32)
        for s in range(seqs):
            row = slice(s, s + 1)
            state = s_ref[s, p]
            sa = jnp.sum(state * (-kk[row]), axis=-1, keepdims=True)
            update = sa * b[row] + v_cols[:, s:s + 1] * k[row]
            state = state * decay[row] + jnp.where(same_head, update, 0.0)
            s_out_ref[s, p] = state
            y_col = jnp.sum(state * r[row], axis=-1, keepdims=True)
            y_cols = jnp.where(sq_lane == s, y_col, y_cols)
        y = jnp.transpose(y_cols)[0:seqs, :]
        mean = head_sums(y) * (1.0 / n)
        d = y - mean
        var = head_sums(d * d) * (1.0 / n)
        yn = d * lax.rsqrt(var + GN_EPS) * lnw_ref[:, ln] + lnb_ref[:, ln]
        bonus = head_sums(r * k * rk_ref[:, ln]) * v
        y_ref[:, ln] = ((yn + bonus) * g_ref[:, ln]).astype(y_ref.dtype)


def rwkv_step(r, lw, k, v, kkr, a, g, lnx_w, lnx_b, r_k, s0_bd, seqs_per_step):
    n_seq = r.shape[0]
    n_pairs = N_HEADS_B // 2
    tile = pl.BlockSpec((seqs_per_step, WIDTH_B), lambda i: (i, 0))
    par = pl.BlockSpec((1, WIDTH_B), lambda i: (0, 0))
    st = pl.BlockSpec((seqs_per_step, n_pairs, LANES, LANES), lambda i: (i, 0, 0, 0))
    return pl.pallas_call(
        _rwkv_step_kernel,
        grid=(n_seq // seqs_per_step,),
        in_specs=[tile] * 7 + [par] * 3 + [st],
        out_specs=[tile, st],
        out_shape=[jax.ShapeDtypeStruct((n_seq, WIDTH_B), BF16),
                   jax.ShapeDtypeStruct((n_seq, n_pairs, LANES, LANES), F32)],
        compiler_params=_cparams(("parallel",)),
        name="rwkv_step",
    )(r, lw, k, v, kkr, a, g, lnx_w.reshape(1, -1), lnx_b.reshape(1, -1), r_k.reshape(1, -1), s0_bd)


def _pair_states_to_bd(s):
    bsz = s.shape[0]
    s = s.reshape(bsz, N_HEADS_B // 2, 2, D_HEAD_B, D_HEAD_B)
    z = jnp.zeros_like(s[:, :, 0])
    top = jnp.concatenate([s[:, :, 0], z], axis=-1)
    bot = jnp.concatenate([z, s[:, :, 1]], axis=-1)
    return jnp.concatenate([top, bot], axis=-2)


def _bd_to_pair_states(s_bd):
    bsz = s_bd.shape[0]
    n = D_HEAD_B
    s0 = s_bd[:, :, 0:n, 0:n]
    s1 = s_bd[:, :, n:2 * n, n:2 * n]
    return jnp.stack([s0, s1], axis=2).reshape(bsz, N_HEADS_B, n, n)


def _merge_kernel(att_ref, rwk_ref, ga_ref, gb_ref, wa_ref, wb_ref, o_ref):
    ya = _dot(att_ref[...], wa_ref[...])
    yb = _dot(rwk_ref[...], wb_ref[...])
    o_ref[...] = (_sigmoid(ga_ref[...]) * ya + _sigmoid(gb_ref[...]) * yb).astype(o_ref.dtype)


def merge_branches(att, rwk, gates, wa, wb, tm, tn):
    m = att.shape[0]
    n = wa.shape[1]
    nb = n // tn
    return pl.pallas_call(
        _merge_kernel,
        grid=(m // tm, nb),
        in_specs=[pl.BlockSpec((tm, WIDTH_A), lambda i, j: (i, 0)),
                  pl.BlockSpec((tm, WIDTH_B), lambda i, j: (i, 0)),
                  pl.BlockSpec((tm, tn), lambda i, j: (i, j)),
                  pl.BlockSpec((tm, tn), lambda i, j: (i, j + nb)),
                  pl.BlockSpec((WIDTH_A, tn), lambda i, j: (0, j)),
                  pl.BlockSpec((WIDTH_B, tn), lambda i, j: (0, j))],
        out_specs=pl.BlockSpec((tm, tn), lambda i, j: (i, j)),
        out_shape=jax.ShapeDtypeStruct((m, n), BF16),
        compiler_params=_cparams(("parallel", "arbitrary")),
        name="merge_branches",
    )(att, rwk, gates, gates, wa, wb)


def _mm_norm_res_kernel(*refs, next_norm):
    if next_norm:
        a_ref, w_ref, x_ref, g_ref, g_next_ref, o_ref, h_ref, acc_ref = refs
    else:
        a_ref, w_ref, x_ref, g_ref, o_ref, acc_ref = refs
    kk = pl.program_id(1)

    @pl.when(kk == 0)
    def _():
        acc_ref[...] = jnp.zeros(acc_ref.shape, F32)

    acc_ref[...] += _dot(a_ref[...], w_ref[...])

    @pl.when(kk == pl.num_programs(1) - 1)
    def _():
        y = x_ref[...] + _rms(acc_ref[...], g_ref[...], NORM_EPS)
        o_ref[...] = y
        if next_norm:
            h_ref[...] = _rms(y, g_next_ref[...], NORM_EPS).astype(h_ref.dtype)


def matmul_norm_residual(a, w, x, g, tm, tk, name, g_next=None):
    m, kdim = a.shape
    n = w.shape[1]
    vec = pl.BlockSpec((1, n), lambda i, k: (0, 0))
    rows = pl.BlockSpec((tm, n), lambda i, k: (i, 0))
    next_norm = g_next is not None
    return pl.pallas_call(
        functools.partial(_mm_norm_res_kernel, next_norm=next_norm),
        grid=(m // tm, kdim // tk),
        in_specs=[pl.BlockSpec((tm, tk), lambda i, k: (i, k)),
                  pl.BlockSpec((tk, n), lambda i, k: (k, 0)),
                  rows, vec] + ([vec] if next_norm else []),
        out_specs=[rows, rows] if next_norm else rows,
        out_shape=([jax.ShapeDtypeStruct((m, n), F32), jax.ShapeDtypeStruct((m, n), BF16)] if next_norm
                   else jax.ShapeDtypeStruct((m, n), F32)),
        scratch_shapes=[pltpu.VMEM((tm, n), F32)],
        compiler_params=_cparams(("parallel", "arbitrary")),
        name=name,
    )(a, w, x, g.reshape(1, n), *([g_next.reshape(1, n)] if next_norm else []))


def _gate_up_kernel(h_ref, wg_ref, wu_ref, o_ref, wg16_ref, wu16_ref):
    @pl.when(pl.program_id(1) == 0)
    def _():
        wg16_ref[...] = wg_ref[...].astype(BF16)
        wu16_ref[...] = wu_ref[...].astype(BF16)

    h = h_ref[...]
    gt = _dot(h, wg16_ref[...])
    up = _dot(h, wu16_ref[...])
    o_ref[...] = (gt * _sigmoid(gt) * up).astype(o_ref.dtype)


def ffn_gate_up(h, wg, wu, tm, tn):
    m, kdim = h.shape
    n = wg.shape[1]
    return pl.pallas_call(
        _gate_up_kernel,
        grid=(n // tn, m // tm),
        in_specs=[pl.BlockSpec((tm, kdim), lambda j, i: (i, 0)),
                  pl.BlockSpec((kdim, tn), lambda j, i: (0, j)),
                  pl.BlockSpec((kdim, tn), lambda j, i: (0, j))],
        out_specs=pl.BlockSpec((tm, tn), lambda j, i: (i, j)),
        out_shape=jax.ShapeDtypeStruct((m, n), BF16),
        scratch_shapes=[pltpu.VMEM((kdim, tn), BF16), pltpu.VMEM((kdim, tn), BF16)],
        compiler_params=_cparams(("parallel", "arbitrary")),
        name="ffn_gate_up",
    )(h, wg, wu)


def _pad_lora_cols(x, axis=-1):
    o = 3 * WIDTH_B
    x = jnp.moveaxis(x, axis, -1)
    z = jnp.zeros(x.shape[:-1] + (LORA_PAD - R_DECAY,), x.dtype)
    x = jnp.concatenate([x[..., :o], x[..., o:o + R_DECAY], z,
                         x[..., o + R_DECAY:o + R_DECAY + R_AAA], z,
                         x[..., o + R_DECAY + R_AAA:]], axis=-1)
    return jnp.moveaxis(x, -1, axis)


def _unpad_lora_cols(x):
    o = 3 * WIDTH_B
    return jnp.concatenate([x[..., :o], x[..., o:o + R_DECAY], x[..., o + LORA_PAD:o + LORA_PAD + R_AAA],
                            x[..., o + 2 * LORA_PAD:]], axis=-1)


def _pad_rows(w, rows):
    return jnp.concatenate([w, jnp.zeros((rows - w.shape[0],) + w.shape[1:], w.dtype)], axis=0)


def _tiles(m):
    return dict(
        tm=min(m, 1024),
        tm_norm=min(m, 512),
        tm_prep=min(m, 256),
        tm_out=min(m, 512),
        tn=1024,
        tn_rw=512,
        tn_ffn=512,
        tk_out=D_MODEL,
        tk_ffn=2816,
        rwkv_pairs=N_HEADS_B // 2,
    )


def _group(x, pos, prev_rows, s0, n_seq, wts, tiles, q_dtype, q_scale, attention_fn):
    tm = tiles["tm"]
    h = rmsnorm_bf16(x, wts["norm_mix_pre"], tiles["tm_norm"])
    tables = rope_tables(pos)
    w_in, tn = wts["w_in"], tiles["tn"]
    (q,) = projection(h, w_in, 0, WIDTH_A, [q_dtype], tm, tn, tables, q_scale, name="proj_q")
    one_sequence = prev_rows is None
    k, *k16 = projection(h, w_in, WIDTH_A, WIDTH_A, [F32, BF16] if one_sequence else [F32], tm, tn, tables,
                         name="proj_k")
    v, *v_t = projection(h, w_in, 2 * WIDTH_A, WIDTH_A, [F32], tm, tn, values_t=one_sequence, name="proj_v")
    (rw,) = projection(h, wts["w_rw"], 0, RW_PAD_COLS, [F32], tm, tiles["tn_rw"], name="proj_rw")
    (gates,) = projection(h, wts["w_gates"], 0, 2 * D_MODEL, [F32], tm, tn, name="proj_gates")

    att = attention_fn(q, k, v, k16, v_t)

    prep = rwkv_prep(rw, prev_rows, wts["mu"], wts["w0"], wts["a0"], wts["k_k"], wts["k_a"],
                     wts["w_decay"], wts["w_aaa"], wts["w_gate"], tiles["tm_prep"])
    if one_sequence:
        rwk, s_bd = rwkv_chunked(*prep, wts["lnx_w"], wts["lnx_b"], wts["r_k"], _pair_states_to_bd(s0), n_seq,
                                 tiles["rwkv_pairs"])
    else:
        rwk, s_bd = rwkv_step(*prep, wts["lnx_w"], wts["lnx_b"], wts["r_k"], _pair_states_to_bd(s0), SUBLANES)
    s_new = _bd_to_pair_states(s_bd)

    mrg = merge_branches(att, rwk, gates, wts["w_branch_a"], wts["w_branch_b"], tm, tiles["tn"])
    x1, hf = matmul_norm_residual(mrg, wts["w_out"], x, wts["norm_mix_post"], tiles["tm_out"], tiles["tk_out"],
                                  "out_proj", g_next=wts["norm_ffn_pre"])
    act = ffn_gate_up(hf, wts["w_ffn_gate"], wts["w_ffn_up"], tm, tiles["tn_ffn"])
    y = matmul_norm_residual(act, wts["w_ffn_down"], x1, wts["norm_ffn_post"], tiles["tm_out"], tiles["tk_ffn"], "ffn_down")
    return y, k, v, rw, s_new


def kernel(x_prompt, x_sample, cache_k, cache_v, state_rwkv, state_shift, page_table, norm_mix_pre, w_in, lambda_q1, lambda_k1, lambda_q2, lambda_k2, subln_w, rw_mu, w0, w_decay, a0, w_aaa, w_gate_lora, k_k, k_a, r_k, lnx_w, lnx_b, w_branch_a, w_branch_b, w_out, norm_mix_post, norm_ffn_pre, w_ffn_gate, w_ffn_up, w_ffn_down, norm_ffn_post):
    depth = w_in.shape[0]
    bsz, seq = x_prompt.shape[0], x_prompt.shape[1]
    db, ds = x_sample.shape[0], x_sample.shape[1]
    assert depth == 1 and bsz == 1 and ds == 1
    n_pool = cache_k.shape[1]
    n_pages = page_table.shape[1]
    past = n_pages * PAGE_SIZE
    l = 0
    lambda_init = 0.8 - 0.6 * math.exp(-0.3 * l)

    wi = w_in[l].T
    row = lambda p: p.reshape(1, -1)
    wts = {
        "norm_mix_pre": norm_mix_pre[l],
        "w_in": wi,
        "w_rw": _pad_lora_cols(wi[A_COLS:A_COLS + RW_COLS], axis=0),
        "w_gates": wi[A_COLS + RW_COLS:],
        "mu": _pad_lora_cols(row(rw_mu[l])),
        "w0": row(w0[l]), "a0": row(a0[l]), "k_k": row(k_k[l]), "k_a": row(k_a[l]),
        "w_decay": _pad_rows(w_decay[l], LORA_PAD).astype(BF16),
        "w_aaa": _pad_rows(w_aaa[l], LORA_PAD).astype(BF16),
        "w_gate": w_gate_lora[l].astype(BF16),
        "lnx_w": lnx_w[l], "lnx_b": lnx_b[l], "r_k": r_k[l],
        "w_branch_a": w_branch_a[l].astype(BF16), "w_branch_b": w_branch_b[l].astype(BF16),
        "w_out": w_out[l].astype(BF16), "norm_mix_post": norm_mix_post[l],
        "norm_ffn_pre": norm_ffn_pre[l],
        "w_ffn_gate": w_ffn_gate[l], "w_ffn_up": w_ffn_up[l],
        "w_ffn_down": w_ffn_down[l].astype(BF16), "norm_ffn_post": norm_ffn_post[l],
    }
    lam_params = [row(p[l]) for p in (lambda_q1, lambda_k1, lambda_q2, lambda_k2)]
    subln = subln_w[l]

    tiles_p = _tiles(seq)

    def prompt_attention(q, k, v, k16, v_t):
        del k, v
        return prompt_diff_attention(q, k16[0], v_t[0], lam_params, subln, lambda_init, 512, 512)

    s0_p = jnp.zeros((bsz, N_HEADS_B, D_HEAD_B, D_HEAD_B), F32)
    y_p, k_p, v_p, rw_p, s_p = _group(x_prompt.reshape(seq, D_MODEL), jnp.arange(seq), None, s0_p, 1,
                                      wts, tiles_p, BF16, LOG2_E * D_HEAD_A ** -0.5, prompt_attention)

    tiles_s = _tiles(db)
    ck = jnp.transpose(cache_k[l], (0, 2, 3, 1)).reshape(n_pool, WIDTH_A, PAGE_SIZE)
    cv = cache_v[l]

    def sample_attention(q, k, v, k16, v_t):
        del k16, v_t
        return sample_diff_attention(q, k, v, ck, cv, page_table, lam_params, subln, lambda_init, 8)

    y_s, k_s, v_s, rw_s, s_s = _group(x_sample.reshape(db, D_MODEL), jnp.full((db,), past, jnp.int32),
                                      _pad_lora_cols(state_shift[l]),
                                      state_rwkv[l], db, wts, tiles_s, F32, D_HEAD_A ** -0.5, sample_attention)

    return (y_p.reshape(bsz, seq, D_MODEL),
            y_s.reshape(db, ds, D_MODEL),
            k_p.reshape(1, bsz, seq, 2 * N_HEADS_A, D_HEAD_A),
            v_p.reshape(1, bsz, seq, N_HEADS_A, 2 * D_HEAD_A),
            s_p.reshape(1, bsz, N_HEADS_B, D_HEAD_B, D_HEAD_B),
            _unpad_lora_cols(rw_p[-1:]).reshape(1, bsz, RW_COLS),
            k_s.reshape(1, db, ds, 2 * N_HEADS_A, D_HEAD_A),
            v_s.reshape(1, db, ds, N_HEADS_A, 2 * D_HEAD_A),
            s_s.reshape(1, db, N_HEADS_B, D_HEAD_B, D_HEAD_B),
            _unpad_lora_cols(rw_s).reshape(1, db, RW_COLS))
```
